```python
import jax, jax.numpy as jnp
from jax import lax
import numpy as np

D_MODEL = 1024
BATCH = 4
SEQ = 8192
DEPTH = 4

N_META = 16
D_CONV = D_MODEL
CONV_WIDTH = 31
D_RWKV = D_MODEL
HEAD_SIZE = 64
N_RWKV_HEADS = D_RWKV // HEAD_SIZE
RANK_W = 64
RANK_A = 64
RANK_G = 128
D_FF = 2816
LN_EPS = 1e-5
GN_EPS = 64e-5
ALPHA = (2 * DEPTH) ** 0.25
BETA = (8 * DEPTH) ** -0.25
RWKV_COLS = 3 * D_RWKV + RANK_W + RANK_A + RANK_G
N_IN = 2 * D_CONV + RWKV_COLS + 2 * D_MODEL

kernel_name = "hybrid_conformer_rwkv7_macaron_deepnorm"


def _layer_norm(x, g, b, eps=LN_EPS):
    xf = x.astype(jnp.float32)
    mu = jnp.mean(xf, axis=-1, keepdims=True)
    var = jnp.mean(jnp.square(xf - mu), axis=-1, keepdims=True)
    y = (xf - mu) * lax.rsqrt(var + eps)
    return (y * g.astype(jnp.float32) + b.astype(jnp.float32)).astype(x.dtype)


def _swiglu(x, wg, wu, wd):
    return (jax.nn.silu(x @ wg) * (x @ wu)) @ wd


def _shift(z):
    return jnp.pad(z, ((0, 0), (1, 0), (0, 0)))[:, :-1]


def _conv_branch(glu_a, glu_b, conv_w, conv_b, ln_g, ln_b, wo):
    u = glu_a * jax.nn.sigmoid(glu_b)
    kern = conv_w[:, None, :].astype(u.dtype)
    u = lax.conv_general_dilated(u, kern, window_strides=(1,),
                                 padding=[(CONV_WIDTH - 1, 0)],
                                 dimension_numbers=("NWC", "WIO", "NWC"),
                                 feature_group_count=D_CONV) + conv_b
    u = jax.nn.silu(_layer_norm(u, ln_g, ln_b))
    return u @ wo


def _rwkv7_scan(r, decay, k, v, kk, a):
    B, L, H, N = r.shape
    xs = tuple(jnp.moveaxis(t, 1, 0) for t in (r, decay, k, v, kk, a))

    def step(S, inp):
        r_t, w_t, k_t, v_t, kk_t, a_t = inp
        sa = jnp.einsum("bhvk,bhk->bhv", S, -kk_t)
        S = (S * w_t[:, :, None, :]
             + sa[..., None] * (kk_t * a_t)[:, :, None, :]
             + v_t[..., None] * k_t[:, :, None, :])
        y = jnp.einsum("bhvk,bhk->bhv", S, r_t)
        return S, y

    S0 = jnp.zeros((B, H, N, N), jnp.float32)
    _, ys = lax.scan(step, S0, xs)
    return jnp.moveaxis(ys, 0, 1)


def _rwkv_branch(zr, time_mix, w0, w_up, a0, a_up, g_up, k_k, k_a, r_k, lnx_g, lnx_b, wo):
    B, L, _ = zr.shape
    H, N = N_RWKV_HEADS, HEAD_SIZE
    zr = zr + (_shift(zr) - zr) * time_mix
    r, k, v, zw, za, zg = jnp.split(
        zr, np.cumsum([D_RWKV, D_RWKV, D_RWKV, RANK_W, RANK_A]).tolist(), axis=-1)
    w = -jax.nn.softplus(-(w0 + jnp.tanh(zw) @ w_up)) - 0.5
    decay = jnp.exp(-jnp.exp(w.astype(jnp.float32)))
    a = jax.nn.sigmoid(a0 + za @ a_up)
    g = jax.nn.sigmoid(zg) @ g_up
    kk = (k * k_k).reshape(B, L, H, N).astype(jnp.float32)
    kk = kk / jnp.maximum(jnp.linalg.norm(kk, axis=-1, keepdims=True), 1e-12)
    k = k * (1.0 + (a - 1.0) * k_a)
    hd = lambda t: t.reshape(B, L, H, N).astype(jnp.float32)
    rh, kh, vh, ah = hd(r), hd(k), hd(v), hd(a)
    y = _rwkv7_scan(rh, decay.reshape(B, L, H, N), kh, vh, kk, ah)
    mu = jnp.mean(y, axis=-1, keepdims=True)
    var = jnp.mean(jnp.square(y - mu), axis=-1, keepdims=True)
    y = ((y - mu) * lax.rsqrt(var + GN_EPS)).reshape(B, L, D_RWKV)
    y = y * lnx_g.astype(jnp.float32) + lnx_b.astype(jnp.float32)
    bonus = jnp.sum(rh * kh * r_k.astype(jnp.float32), axis=-1, keepdims=True) * vh
    y = (y + bonus.reshape(B, L, D_RWKV)).astype(zr.dtype) * g
    return y @ wo


def _mixer(x, w_in, time_mix, conv_w, conv_b, conv_ln_g, conv_ln_b, conv_wo,
           w0, w_up, a0, a_up, g_up, k_k, k_a, r_k, lnx_g, lnx_b, rwkv_wo, w_out):
    z = x @ w_in
    glu_a, glu_b, zr, gate_c, gate_r = jnp.split(
        z, np.cumsum([D_CONV, D_CONV, RWKV_COLS, D_MODEL]).tolist(), axis=-1)
    yc = _conv_branch(glu_a, glu_b, conv_w, conv_b, conv_ln_g, conv_ln_b, conv_wo)
    yr = _rwkv_branch(zr, time_mix, w0, w_up, a0, a_up, g_up, k_k, k_a, r_k,
                      lnx_g, lnx_b, rwkv_wo)
    h = jax.nn.sigmoid(gate_c) * yc + jax.nn.sigmoid(gate_r) * yr
    return h @ w_out


def setup_inputs(seed: int = 0) -> dict:
    key = jax.random.key(seed)
    ks = iter(jax.random.split(key, 40))
    nrm = lambda shape, s: jax.random.normal(next(ks), shape, jnp.float32) * s
    gain = lambda shape: 1.0 + nrm(shape, 0.02)
    D, F, Ld = D_MODEL, D_FF, DEPTH
    return {
        "x": nrm((BATCH, SEQ, D), 1.0),
        "meta_tokens": nrm((N_META, D), 1.0),
        "ln1_g": gain((Ld, D)), "ln1_b": nrm((Ld, D), 0.02),
        "ffn1_wg": nrm((Ld, D, F), D ** -0.5),
        "ffn1_wu": nrm((Ld, D, F), D ** -0.5),
        "ffn1_wd": nrm((Ld, F, D), F ** -0.5 * BETA),
        "w_in": nrm((Ld, D, N_IN), D ** -0.5),
        "time_mix": jax.random.uniform(next(ks), (Ld, RWKV_COLS), jnp.float32),
        "conv_w": nrm((Ld, CONV_WIDTH, D_CONV), CONV_WIDTH ** -0.5),
        "conv_b": nrm((Ld, D_CONV), 0.02),
        "conv_ln_g": gain((Ld, D_CONV)), "conv_ln_b": nrm((Ld, D_CONV), 0.02),
        "conv_wo": nrm((Ld, D_CONV, D), D_CONV ** -0.5),
        "w0": jax.random.uniform(next(ks), (Ld, D_RWKV), jnp.float32, -6.0, 1.0),
        "w_up": nrm((Ld, RANK_W, D_RWKV), 0.5 * RANK_W ** -0.5),
        "a0": nrm((Ld, D_RWKV), 0.1),
        "a_up": nrm((Ld, RANK_A, D_RWKV), RANK_A ** -0.5),
        "g_up": nrm((Ld, RANK_G, D_RWKV), RANK_G ** -0.5),
        "k_k": 0.85 + nrm((Ld, D_RWKV), 0.02),
        "k_a": gain((Ld, D_RWKV)),
        "r_k": nrm((Ld, N_RWKV_HEADS, HEAD_SIZE), 0.1),
        "lnx_g": gain((Ld, D_RWKV)), "lnx_b": nrm((Ld, D_RWKV), 0.02),
        "rwkv_wo": nrm((Ld, D_RWKV, D), D_RWKV ** -0.5),
        "w_out": nrm((Ld, D, D), D ** -0.5 * BETA),
        "ln2_g": gain((Ld, D)), "ln2_b": nrm((Ld, D), 0.02),
        "ffn2_wg": nrm((Ld, D, F), D ** -0.5),
        "ffn2_wu": nrm((Ld, D, F), D ** -0.5),
        "ffn2_wd": nrm((Ld, F, D), F ** -0.5 * BETA),
        "ln3_g": gain((Ld, D)), "ln3_b": nrm((Ld, D), 0.02),
    }


def reference(x, meta_tokens, ln1_g, ln1_b, ffn1_wg, ffn1_wu, ffn1_wd, w_in, time_mix,
              conv_w, conv_b, conv_ln_g, conv_ln_b, conv_wo, w0, w_up, a0, a_up, g_up,
              k_k, k_a, r_k, lnx_g, lnx_b, rwkv_wo, w_out, ln2_g, ln2_b,
              ffn2_wg, ffn2_wu, ffn2_wd, ln3_g, ln3_b):
    B = x.shape[0]
    meta = jnp.broadcast_to(meta_tokens.astype(x.dtype)[None], (B, N_META, D_MODEL))
    h = jnp.concatenate([meta, x], axis=1)
    for l in range(DEPTH):
        h = _layer_norm(ALPHA * h + 0.5 * _swiglu(h, ffn1_wg[l], ffn1_wu[l], ffn1_wd[l]),
                        ln1_g[l], ln1_b[l])
        m = _mixer(h, w_in[l], time_mix[l], conv_w[l], conv_b[l], conv_ln_g[l],
                   conv_ln_b[l], conv_wo[l], w0[l], w_up[l], a0[l], a_up[l], g_up[l],
                   k_k[l], k_a[l], r_k[l], lnx_g[l], lnx_b[l], rwkv_wo[l], w_out[l])
        h = _layer_norm(ALPHA * h + m, ln2_g[l], ln2_b[l])
        h = _layer_norm(ALPHA * h + 0.5 * _swiglu(h, ffn2_wg[l], ffn2_wu[l], ffn2_wd[l]),
                        ln3_g[l], ln3_b[l])
    return h[:, N_META:]
```

```python
import functools

import jax
import jax.numpy as jnp
from jax import lax
from jax.experimental import pallas as pl
from jax.experimental.pallas import tpu as pltpu

LN_EPS = 1e-5
GN_EPS = 64e-5
KK_NORM_FLOOR = 1e-12

F32 = jnp.float32
MXU_DTYPE = jnp.bfloat16
ACT_DTYPE = jnp.bfloat16

V7X_LANES = 128
V7X_MXU_EDGE = 256
V7X_VMEM_BYTES = 64 * 1024 * 1024
VMEM_LIMIT_BYTES = 56 * 1024 * 1024

RWKV_CHUNK = 64
CONV_HALO = 32
CONV_ROWS = 16


def _dot(a, b):
    return jnp.dot(a, b, preferred_element_type=F32)


def _dot_nt(a, b):
    return lax.dot_general(a, b, (((1,), (1,)), ((), ())), preferred_element_type=F32)


def _dot_tn(a, b):
    return lax.dot_general(a, b, (((0,), (0,)), ((), ())), preferred_element_type=F32)


def _mx(x):
    return x.astype(MXU_DTYPE)


def _split_terms(x, n):
    terms, rem = [], x
    for i in range(n):
        t = rem.astype(MXU_DTYPE)
        terms.append(t)
        if i + 1 < n:
            rem = rem - t.astype(F32)
    return terms


def _layer_norm(x, g, b, eps=LN_EPS):
    mu = jnp.mean(x, axis=-1, keepdims=True)
    xc = x - mu
    var = jnp.mean(xc * xc, axis=-1, keepdims=True)
    return xc * lax.rsqrt(var + eps) * g + b


def _softplus(x):
    return jnp.maximum(x, 0.0) + jnp.log1p(jnp.exp(-jnp.abs(x)))


def _largest_tile(n, cap, mult):
    best = None
    for t in range(mult, min(n, cap) + 1, mult):
        if n % t == 0:
            best = t
    assert best is not None, (n, cap, mult)
    return best


def _resident(shape):
    nd = len(shape)
    return pl.BlockSpec(shape, lambda *_: (0,) * nd, pipeline_mode=pl.Buffered(1))


def _params(sem):
    return pltpu.CompilerParams(dimension_semantics=sem, vmem_limit_bytes=VMEM_LIMIT_BYTES)


def _ffn_kernel(h_ref, wg_ref, wu_ref, wd_ref, g_ref, b_ref, o_ref, *, alpha):
    x = h_ref[...]
    xb = _mx(x)
    gate = _dot(xb, wg_ref[...])
    up = _dot(xb, wu_ref[...])
    act = _mx(gate * jax.nn.sigmoid(gate) * up)
    y = _dot(act, wd_ref[...])
    o_ref[...] = _layer_norm(alpha * x + 0.5 * y, g_ref[...], b_ref[...])


def _ffn_ln(h2d, wg, wu, wd, g, b, *, alpha, tm):
    T, D = h2d.shape
    F = wg.shape[1]
    row = pl.BlockSpec((tm, D), lambda i: (i, 0))
    return pl.pallas_call(
        functools.partial(_ffn_kernel, alpha=alpha),
        grid=(T // tm,),
        in_specs=[row, _resident((D, F)), _resident((D, F)), _resident((F, D)),
                  _resident((1, D)), _resident((1, D))],
        out_specs=row,
        out_shape=jax.ShapeDtypeStruct((T, D), F32),
        compiler_params=_params(("parallel",)),
        name="ffn_ln",
    )(h2d, wg, wu, wd, g, b)


def _head_sum(x, bd_ones, n_terms=2):
    Q = bd_ones.shape[0]
    outs = []
    for q in range(x.shape[1] // Q):
        xq = x[:, q * Q:(q + 1) * Q]
        acc = None
        for t in _split_terms(xq, n_terms):
            p = _dot(t, bd_ones)
            acc = p if acc is None else acc + p
        outs.append(acc)
    return outs[0] if len(outs) == 1 else jnp.concatenate(outs, axis=1)


def _bd_mask(Q, hs):
    row = lax.broadcasted_iota(jnp.int32, (Q, Q), 0)
    col = lax.broadcasted_iota(jnp.int32, (Q, Q), 1)
    return (row // hs) == (col // hs)


def _mixer_in_kernel(h_ref, win_ref, tmix_ref, cw_ref, cb_ref, clg_ref, clb_ref, cwo_ref,
                     wabd_ref, gup_ref, w0_ref, a0_ref, kkw_ref, kaw_ref,
                     pc_ref, r_ref, lw_ref, k_ref, v_ref, kk_ref, b_ref, g_ref, sgr_ref,
                     ubuf, cbuf, zprev, *, d_conv, d_rwkv, rank_w, rank_wa, rank_g, hs):
    TM = h_ref.shape[0]
    W = cw_ref.shape[0]
    t = pl.program_id(1)

    @pl.when(t == 0)
    def _():
        ubuf[0:CONV_HALO, :] = jnp.zeros((CONV_HALO, d_conv), F32)
        zprev[...] = jnp.zeros_like(zprev)

    hb = _mx(h_ref[...])
    c0 = 2 * d_conv
    c1 = c0 + 3 * d_rwkv + rank_wa + rank_g
    c2 = c1 + d_conv

    ab = _dot(hb, win_ref[:, 0:c0])
    ubuf[CONV_HALO:CONV_HALO + TM, :] = ab[:, :d_conv] * jax.nn.sigmoid(ab[:, d_conv:])
    base = CONV_HALO - (W - 1)
    for i in range(TM // CONV_ROWS):
        acc = None
        for j in range(W):
            s = base + j + i * CONV_ROWS
            term = ubuf[s:s + CONV_ROWS, :] * cw_ref[j:j + 1, :]
            acc = term if acc is None else acc + term
        cbuf[i * CONV_ROWS:(i + 1) * CONV_ROWS, :] = acc
    ubuf[0:CONV_HALO, :] = ubuf[TM:TM + CONV_HALO, :]
    cn = _layer_norm(cbuf[...] + cb_ref[...], clg_ref[...], clb_ref[...])
    cs = _mx(cn * jax.nn.sigmoid(cn))
    yc = _dot(cs, cwo_ref[...])
    gate_c = _dot(hb, win_ref[:, c1:c2])
    pc_ref[...] = (jax.nn.sigmoid(gate_c) * yc).astype(pc_ref.dtype)
    gate_r = _dot(hb, win_ref[:, c2:c2 + d_conv])
    sgr_ref[...] = jax.nn.sigmoid(gate_r).astype(sgr_ref.dtype)

    zr = _dot(hb, win_ref[:, c0:c1])
    rows = lax.broadcasted_iota(jnp.int32, zr.shape, 0)
    prev = jnp.where(rows == 0, zprev[7:8, :], pltpu.roll(zr, 1, 0))
    zprev[...] = zr[TM - 8:TM, :]
    zr = zr + (prev - zr) * tmix_ref[...]
    r = zr[:, 0:d_rwkv]
    k = zr[:, d_rwkv:2 * d_rwkv]
    v = zr[:, 2 * d_rwkv:3 * d_rwkv]
    zwa = zr[:, 3 * d_rwkv:3 * d_rwkv + rank_wa]
    zg = zr[:, 3 * d_rwkv + rank_wa:3 * d_rwkv + rank_wa + rank_g]
    lanes = lax.broadcasted_iota(jnp.int32, zwa.shape, 1)
    xwa = jnp.where(lanes < rank_w, jnp.tanh(zwa), zwa)
    lora = _dot(_mx(xwa), wabd_ref[...])
    g = _dot(_mx(jax.nn.sigmoid(zg)), gup_ref[...])
    w = -_softplus(-(w0_ref[...] + lora[:, :d_rwkv])) - 0.5
    lw_ref[...] = -jnp.exp(w)
    a = jax.nn.sigmoid(a0_ref[...] + lora[:, d_rwkv:])
    kk = k * kkw_ref[...]
    bd_ones = _bd_mask(V7X_MXU_EDGE, hs).astype(MXU_DTYPE)
    kk = kk / jnp.maximum(jnp.sqrt(_head_sum(kk * kk, bd_ones)), KK_NORM_FLOOR)
    r_ref[...] = r.astype(r_ref.dtype)
    k_ref[...] = (k * (1.0 + (a - 1.0) * kaw_ref[...])).astype(k_ref.dtype)
    v_ref[...] = v.astype(v_ref.dtype)
    kk_ref[...] = kk.astype(kk_ref.dtype)
    b_ref[...] = (kk * a).astype(b_ref.dtype)
    g_ref[...] = g.astype(g_ref.dtype)


def _mixer_in(h3d, win, tmix, cw, cb, clg, clb, cwo, wabd, gup, w0, a0, kkw, kaw, *, tm, hs, rank_w):
    B, Lp, D = h3d.shape
    d_conv = cwo.shape[0]
    d_rwkv = w0.shape[1]
    rank_wa = wabd.shape[0]
    rank_g = gup.shape[0]
    W = cw.shape[0]
    assert W - 1 <= CONV_HALO and tm % CONV_ROWS == 0 and tm >= CONV_HALO
    assert rank_wa % V7X_LANES == 0 and rank_g % V7X_LANES == 0 and d_rwkv % V7X_MXU_EDGE == 0
    n_zr = 3 * d_rwkv + rank_wa + rank_g
    tile = lambda d: pl.BlockSpec((None, tm, d), lambda b, t: (b, t, 0))
    act = lambda d: jax.ShapeDtypeStruct((B, Lp, d), ACT_DTYPE)
    kern = functools.partial(_mixer_in_kernel, d_conv=d_conv, d_rwkv=d_rwkv, rank_w=rank_w,
                             rank_wa=rank_wa, rank_g=rank_g, hs=hs)
    return pl.pallas_call(
        kern,
        grid=(B, Lp // tm),
        in_specs=[tile(D), _resident(win.shape), _resident(tmix.shape), _resident(cw.shape),
                  _resident(cb.shape), _resident(clg.shape), _resident(clb.shape), _resident(cwo.shape),
                  _resident(wabd.shape), _resident(gup.shape), _resident(w0.shape), _resident(a0.shape),
                  _resident(kkw.shape), _resident(kaw.shape)],
        out_specs=[tile(D), tile(d_rwkv), tile(d_rwkv), tile(d_rwkv), tile(d_rwkv), tile(d_rwkv),
                   tile(d_rwkv), tile(d_rwkv), tile(D)],
        out_shape=[act(D), act(d_rwkv), jax.ShapeDtypeStruct((B, Lp, d_rwkv), F32), act(d_rwkv),
                   act(d_rwkv), act(d_rwkv), act(d_rwkv), act(d_rwkv), act(D)],
        scratch_shapes=[pltpu.VMEM((CONV_HALO + tm, d_conv), F32),
                        pltpu.VMEM((tm, d_conv), F32),
                        pltpu.VMEM((8, n_zr), F32)],
        compiler_params=_params(("parallel", "arbitrary")),
        name="mixer_in",
    )(h3d, win, tmix, cw, cb, clg, clb, cwo, wabd, gup, w0, a0, kkw, kaw)


def _rwkv_kernel(r_ref, lw_ref, k_ref, v_ref, kk_ref, b_ref, g_ref, rk_ref, lg_ref, lb_ref,
                 o_ref, s_ref, *, hs):
    C, D = r_ref.shape
    Q = V7X_MXU_EDGE
    HPQ = Q // hs

    @pl.when(pl.program_id(1) == 0)
    def _():
        s_ref[...] = jnp.zeros_like(s_ref)

    r = r_ref[...].astype(F32)
    k = k_ref[...].astype(F32)
    v = v_ref[...].astype(F32)
    kk = kk_ref[...].astype(F32)
    b = b_ref[...].astype(F32)
    lw = lw_ref[...]

    ti = lax.broadcasted_iota(jnp.int32, (C, C), 0)
    si = lax.broadcasted_iota(jnp.int32, (C, C), 1)
    tri = (si <= ti).astype(MXU_DTYPE)
    cum = None
    for term in _split_terms(lw, 3):
        p = _dot(tri, term)
        cum = p if cum is None else cum + p
    cum_last = cum[C - 1:C, :]
    e_in = jnp.exp(cum)
    e_neg = jnp.exp(-cum)
    e_rel = jnp.exp(cum_last - cum)
    a_t = _mx(-kk * jnp.exp(cum - lw))
    r_t = _mx(r * e_in)
    b_t = _mx(b * e_neg)
    k_t = _mx(k * e_neg)
    b_g = _mx(b * e_rel)
    k_g = _mx(k * e_rel)
    g_c = jnp.exp(cum_last)
    v_m = _mx(v)

    bd = _bd_mask(Q, hs)
    trow = lax.broadcasted_iota(jnp.int32, (C, Q), 0)
    spos = lax.broadcasted_iota(jnp.int32, (C, Q), 1) % C
    strict = spos < trow
    incl = spos <= trow
    eye_cat = (spos == trow).astype(F32)
    zero = jnp.zeros((), MXU_DTYPE)

    def bdt(x):
        return jnp.where(bd, jnp.concatenate([_mx(x)] * HPQ, axis=0), zero)

    ys = []
    for q in range(D // Q):
        sl = slice(q * Q, (q + 1) * Q)
        s_bd = s_ref[q]
        s_m = _mx(s_bd)
        lhs = jnp.concatenate([a_t[:, sl], r_t[:, sl]], axis=0)
        a_b = _dot_nt(lhs, bdt(b_t[:, sl]))
        a_k = _dot_nt(lhs, bdt(k_t[:, sl]))
        n_ab = jnp.where(strict, a_b[:C], 0.0)
        a_ak = jnp.where(strict, a_k[:C], 0.0)
        a_rb = jnp.where(incl, a_b[C:], 0.0)
        a_rk = jnp.where(incl, a_k[C:], 0.0)
        p = eye_cat + n_ab
        npow = _dot(_mx(n_ab), bdt(n_ab))
        levels = C.bit_length() - 2
        for lvl in range(levels):
            rhs_bd = bdt(npow)
            if lvl + 1 < levels:
                both = _dot(jnp.concatenate([_mx(p), _mx(npow)], axis=0), rhs_bd)
                p = p + both[:C]
                npow = both[C:]
            else:
                p = p + _dot(_mx(p), rhs_bd)
        v_bd = bdt(v_m[:, sl])
        rhs = _dot_nt(a_t[:, sl], s_m) + _dot(_mx(a_ak), v_bd)
        u = _dot(_mx(p), bdt(rhs))
        u_m = _mx(u)
        y = _dot_nt(r_t[:, sl], s_m) + _dot(_mx(a_rb), bdt(u_m)) + _dot(_mx(a_rk), v_bd)
        upd = _dot_tn(jnp.concatenate([u_m, v_m[:, sl]], axis=0),
                      jnp.concatenate([b_g[:, sl], k_g[:, sl]], axis=0))
        s_ref[q] = s_bd * g_c[:, sl] + jnp.where(bd, upd, 0.0)
        ys.append(y)
    y = ys[0] if len(ys) == 1 else jnp.concatenate(ys, axis=1)

    bd_ones = bd.astype(MXU_DTYPE)
    mu = _head_sum(y, bd_ones) * (1.0 / hs)
    yc = y - mu
    var = _head_sum(yc * yc, bd_ones) * (1.0 / hs)
    yn = yc * lax.rsqrt(var + GN_EPS) * lg_ref[...] + lb_ref[...]
    bonus = _head_sum(r * k * rk_ref[...], bd_ones) * v
    o_ref[...] = ((yn + bonus) * g_ref[...].astype(F32)).astype(o_ref.dtype)


def _rwkv_scan(r, lw, k, v, kk, b, g, rk, lg, lb, *, hs):
    B, Lp, D = r.shape
    C = RWKV_CHUNK
    assert C == hs and V7X_MXU_EDGE % hs == 0 and D % V7X_MXU_EDGE == 0 and Lp % C == 0
    tile = pl.BlockSpec((None, C, D), lambda bi, c: (bi, c, 0))
    return pl.pallas_call(
        functools.partial(_rwkv_kernel, hs=hs),
        grid=(B, Lp // C),
        in_specs=[tile] * 7 + [_resident((1, D))] * 3,
        out_specs=tile,
        out_shape=jax.ShapeDtypeStruct((B, Lp, D), ACT_DTYPE),
        scratch_shapes=[pltpu.VMEM((D // V7X_MXU_EDGE, V7X_MXU_EDGE, V7X_MXU_EDGE), F32)],
        compiler_params=_params(("parallel", "arbitrary")),
        name="rwkv_scan",
    )(r, lw, k, v, kk, b, g, rk, lg, lb)


def _mixer_out_kernel(h_ref, yo_ref, pc_ref, sgr_ref, rwo_ref, wout_ref, g_ref, b_ref, o_ref, *, alpha):
    yr = _dot(_mx(yo_ref[...]), rwo_ref[...])
    hm = pc_ref[...].astype(F32) + sgr_ref[...].astype(F32) * yr
    m = _dot(_mx(hm), wout_ref[...])
    o_ref[...] = _layer_norm(alpha * h_ref[...] + m, g_ref[...], b_ref[...])


def _mixer_out(h2d, yo, pc, sgr, rwo, wout, g, b, *, alpha, tm):
    T, D = h2d.shape
    Dr = yo.shape[1]
    row = lambda d: pl.BlockSpec((tm, d), lambda i: (i, 0))
    return pl.pallas_call(
        functools.partial(_mixer_out_kernel, alpha=alpha),
        grid=(T // tm,),
        in_specs=[row(D), row(Dr), row(D), row(D), _resident(rwo.shape), _resident(wout.shape),
                  _resident((1, D)), _resident((1, D))],
        out_specs=row(D),
        out_shape=jax.ShapeDtypeStruct((T, D), F32),
        compiler_params=_params(("parallel",)),
        name="mixer_out",
    )(h2d, yo, pc, sgr, rwo, wout, g, b)


def kernel(x, meta_tokens, ln1_g, ln1_b, ffn1_wg, ffn1_wu, ffn1_wd, w_in, time_mix, conv_w, conv_b, conv_ln_g, conv_ln_b, conv_wo, w0, w_up, a0, a_up, g_up, k_k, k_a, r_k, lnx_g, lnx_b, rwkv_wo, w_out, ln2_g, ln2_b, ffn2_wg, ffn2_wu, ffn2_wd, ln3_g, ln3_b):
    B, L, D = x.shape
    depth = ffn1_wg.shape[0]
    n_meta = meta_tokens.shape[0]
    hs = r_k.shape[2]
    d_rwkv = w0.shape[1]
    rank_w = w_up.shape[1]
    alpha = (2 * depth) ** 0.25

    Lt = n_meta + L
    Lp = -(-Lt // (2 * RWKV_CHUNK)) * (2 * RWKV_CHUNK)
    tm_time = _largest_tile(Lp, 320, RWKV_CHUNK)
    tm_tok = _largest_tile(B * Lp, 512, 8)

    meta = jnp.broadcast_to(meta_tokens.astype(x.dtype)[None], (B, n_meta, D))
    h = jnp.concatenate([meta, x, jnp.zeros((B, Lp - Lt, D), x.dtype)], axis=1)

    row = lambda p: p.reshape(1, -1).astype(F32)
    zeros_wa = lambda m: jnp.zeros_like(m)
    for l in range(depth):
        h = _ffn_ln(h.reshape(B * Lp, D), _mx(ffn1_wg[l]), _mx(ffn1_wu[l]), _mx(ffn1_wd[l]),
                    row(ln1_g[l]), row(ln1_b[l]), alpha=alpha, tm=tm_tok)
        wabd = jnp.concatenate(
            [jnp.concatenate([w_up[l], zeros_wa(w_up[l])], axis=1),
             jnp.concatenate([zeros_wa(a_up[l]), a_up[l]], axis=1)], axis=0)
        pc, r, lw, k, v, kk, b, g, sgr = _mixer_in(
            h.reshape(B, Lp, D), _mx(w_in[l]), row(time_mix[l]), conv_w[l].astype(F32), row(conv_b[l]),
            row(conv_ln_g[l]), row(conv_ln_b[l]), _mx(conv_wo[l]), _mx(wabd), _mx(g_up[l]),
            row(w0[l]), row(a0[l]), row(k_k[l]), row(k_a[l]), tm=tm_time, hs=hs, rank_w=rank_w)
        yo = _rwkv_scan(r, lw, k, v, kk, b, g, row(r_k[l]), row(lnx_g[l]), row(lnx_b[l]), hs=hs)
        h = _mixer_out(h, yo.reshape(B * Lp, d_rwkv), pc.reshape(B * Lp, D), sgr.reshape(B * Lp, D),
                       _mx(rwkv_wo[l]), _mx(w_out[l]), row(ln2_g[l]), row(ln2_b[l]), alpha=alpha, tm=tm_tok)
        h = _ffn_ln(h, _mx(ffn2_wg[l]), _mx(ffn2_wu[l]), _mx(ffn2_wd[l]),
                    row(ln3_g[l]), row(ln3_b[l]), alpha=alpha, tm=tm_tok)
    return h.reshape(B, Lp, D)[:, n_meta:n_meta + L]
```

```python
import functools

import jax
import jax.numpy as jnp
from jax import lax
from jax.experimental import pallas as pl
from jax.experimental.pallas import tpu as pltpu

LN_EPS = 1e-5
GN_EPS = 64e-5
KK_NORM_FLOOR = 1e-12

F32 = jnp.float32
MXU_DTYPE = jnp.bfloat16
ACT_DTYPE = jnp.bfloat16

V7X_LANES = 128
V7X_SUBLANES = 8
V7X_MXU_EDGE = 256
V7X_VMEM_BYTES = 64 * 1024 * 1024
VMEM_LIMIT_BYTES = 56 * 1024 * 1024

RWKV_CHUNK = 64
RWKV_TILE = 320
CONV_HALO = 32
CONV_ROWS = 64
CONV_LANES = 256


def _dot(a, b):
    return jnp.dot(a, b, preferred_element_type=F32)


def _dot_nt(a, b):
    return lax.dot_general(a, b, (((1,), (1,)), ((), ())), preferred_element_type=F32)


def _dot_tn(a, b):
    return lax.dot_general(a, b, (((0,), (0,)), ((), ())), preferred_element_type=F32)


def _mx(x):
    return x.astype(MXU_DTYPE)


def _split_terms(x, n):
    terms, rem = [], x
    for i in range(n):
        t = rem.astype(MXU_DTYPE)
        terms.append(t)
        if i + 1 < n:
            rem = rem - t.astype(F32)
    return terms


def _layer_norm(x, g, b, eps=LN_EPS):
    mu = jnp.mean(x, axis=-1, keepdims=True)
    xc = x - mu
    var = jnp.mean(xc * xc, axis=-1, keepdims=True)
    return xc * lax.rsqrt(var + eps) * g + b


def _softplus(x):
    return jnp.maximum(x, 0.0) + jnp.log1p(jnp.exp(-jnp.abs(x)))


def _largest_tile(n, cap, mult):
    best = None
    for t in range(mult, min(n, cap) + 1, mult):
        if n % t == 0:
            best = t
    assert best is not None, (n, cap, mult)
    return best


def _resident(shape):
    nd = len(shape)
    return pl.BlockSpec(shape, lambda *_: (0,) * nd, pipeline_mode=pl.Buffered(1))


def _params(sem):
    return pltpu.CompilerParams(dimension_semantics=sem, vmem_limit_bytes=VMEM_LIMIT_BYTES)


def _ffn_kernel(h_ref, wg_ref, wu_ref, wd_ref, g_ref, b_ref, o_ref, *, alpha):
    x = h_ref[...]
    xb = _mx(x)
    gate = _dot(xb, wg_ref[...])
    up = _dot(xb, wu_ref[...])
    act = _mx(gate * jax.nn.sigmoid(gate) * up)
    y = _dot(act, wd_ref[...])
    o_ref[...] = _layer_norm(alpha * x + 0.5 * y, g_ref[...], b_ref[...])


def _ffn_ln(h2d, wg, wu, wd, g, b, *, alpha, tm):
    T, D = h2d.shape
    F = wg.shape[1]
    row = pl.BlockSpec((tm, D), lambda i: (i, 0))
    return pl.pallas_call(
        functools.partial(_ffn_kernel, alpha=alpha),
        grid=(T // tm,),
        in_specs=[row, _resident((D, F)), _resident((D, F)), _resident((F, D)),
                  _resident((1, D)), _resident((1, D))],
        out_specs=row,
        out_shape=jax.ShapeDtypeStruct((T, D), F32),
        compiler_params=_params(("parallel",)),
        name="ffn_ln",
    )(h2d, wg, wu, wd, g, b)


def _head_sum(x, bd_ones, n_terms=2):
    Q = bd_ones.shape[0]
    outs = []
    for q in range(x.shape[1] // Q):
        xq = x[:, q * Q:(q + 1) * Q]
        acc = None
        for t in _split_terms(xq, n_terms):
            p = _dot(t, bd_ones)
            acc = p if acc is None else acc + p
        outs.append(acc)
    return outs[0] if len(outs) == 1 else jnp.concatenate(outs, axis=1)


def _bd_mask(Q, hs):
    row = lax.broadcasted_iota(jnp.int32, (Q, Q), 0)
    col = lax.broadcasted_iota(jnp.int32, (Q, Q), 1)
    return (row // hs) == (col // hs)


def _mixer_in_kernel(h_ref, win_ref, tmix_ref, cw_ref, cb_ref, clg_ref, clb_ref, cwo_ref,
                     wabd_ref, gup_ref, w0_ref, a0_ref, kkw_ref, kaw_ref,
                     pc_ref, r_ref, lw_ref, k_ref, v_ref, kk_ref, b_ref, g_ref, sgr_ref,
                     ubuf, shbuf, cbuf, zprev, *, d_conv, d_rwkv, rank_w, rank_wa, rank_g, hs):
    TM = h_ref.shape[0]
    W = cw_ref.shape[0]
    t = pl.program_id(1)

    @pl.when(t == 0)
    def _():
        ubuf[0:CONV_HALO, :] = jnp.zeros((CONV_HALO, d_conv), F32)
        zprev[...] = jnp.zeros_like(zprev)

    hb = _mx(h_ref[...])
    c0 = 2 * d_conv
    c1 = c0 + 3 * d_rwkv + rank_wa + rank_g
    c2 = c1 + d_conv

    ab = _dot(hb, win_ref[:, 0:c0])
    ubuf[CONV_HALO:CONV_HALO + TM, :] = ab[:, :d_conv] * jax.nn.sigmoid(ab[:, d_conv:])
    base = CONV_HALO - (W - 1)
    n_rows = CONV_HALO + TM
    for g in range(d_conv // CONV_LANES):
        ln = slice(g * CONV_LANES, (g + 1) * CONV_LANES)
        window = ubuf[:, ln]
        for s in range(1, V7X_SUBLANES):
            shbuf[s - 1] = pltpu.roll(window, n_rows - s, 0)
        for i in range(TM // CONV_ROWS):
            acc = None
            for j in range(W):
                s = (base + j) % V7X_SUBLANES
                o = base + j - s + i * CONV_ROWS
                rows_j = ubuf[o:o + CONV_ROWS, ln] if s == 0 else shbuf[s - 1, o:o + CONV_ROWS, :]
                term = rows_j * cw_ref[j:j + 1, ln]
                acc = term if acc is None else acc + term
            cbuf[i * CONV_ROWS:(i + 1) * CONV_ROWS, ln] = acc
    ubuf[0:CONV_HALO, :] = ubuf[TM:TM + CONV_HALO, :]
    cn = _layer_norm(cbuf[...] + cb_ref[...], clg_ref[...], clb_ref[...])
    cs = _mx(cn * jax.nn.sigmoid(cn))
    yc = _dot(cs, cwo_ref[...])
    gate_c = _dot(hb, win_ref[:, c1:c2])
    pc_ref[...] = (jax.nn.sigmoid(gate_c) * yc).astype(pc_ref.dtype)
    gate_r = _dot(hb, win_ref[:, c2:c2 + d_conv])
    sgr_ref[...] = jax.nn.sigmoid(gate_r).astype(sgr_ref.dtype)

    zr = _dot(hb, win_ref[:, c0:c1])
    rows = lax.broadcasted_iota(jnp.int32, zr.shape, 0)
    prev = jnp.where(rows == 0, zprev[7:8, :], pltpu.roll(zr, 1, 0))
    zprev[...] = zr[TM - 8:TM, :]
    zr = zr + (prev - zr) * tmix_ref[...]
    r = zr[:, 0:d_rwkv]
    k = zr[:, d_rwkv:2 * d_rwkv]
    v = zr[:, 2 * d_rwkv:3 * d_rwkv]
    zwa = zr[:, 3 * d_rwkv:3 * d_rwkv + rank_wa]
    zg = zr[:, 3 * d_rwkv + rank_wa:3 * d_rwkv + rank_wa + rank_g]
    lanes = lax.broadcasted_iota(jnp.int32, zwa.shape, 1)
    xwa = jnp.where(lanes < rank_w, jnp.tanh(zwa), zwa)
    lora = _dot(_mx(xwa), wabd_ref[...])
    g = _dot(_mx(jax.nn.sigmoid(zg)), gup_ref[...])
    w = -_softplus(-(w0_ref[...] + lora[:, :d_rwkv])) - 0.5
    lw_ref[...] = -jnp.exp(w)
    a = jax.nn.sigmoid(a0_ref[...] + lora[:, d_rwkv:])
    kk = k * kkw_ref[...]
    bd_ones = _bd_mask(V7X_MXU_EDGE, hs).astype(MXU_DTYPE)
    kk = kk / jnp.maximum(jnp.sqrt(_head_sum(kk * kk, bd_ones)), KK_NORM_FLOOR)
    r_ref[...] = r.astype(r_ref.dtype)
    k_ref[...] = (k * (1.0 + (a - 1.0) * kaw_ref[...])).astype(k_ref.dtype)
    v_ref[...] = v.astype(v_ref.dtype)
    kk_ref[...] = kk.astype(kk_ref.dtype)
    b_ref[...] = (kk * a).astype(b_ref.dtype)
    g_ref[...] = g.astype(g_ref.dtype)


def _mixer_in(h3d, win, tmix, cw, cb, clg, clb, cwo, wabd, gup, w0, a0, kkw, kaw, *, tm, hs, rank_w):
    B, Lp, D = h3d.shape
    d_conv = cwo.shape[0]
    d_rwkv = w0.shape[1]
    rank_wa = wabd.shape[0]
    rank_g = gup.shape[0]
    W = cw.shape[0]
    assert W - 1 <= CONV_HALO and tm % CONV_ROWS == 0 and tm >= CONV_HALO and d_conv % CONV_LANES == 0
    assert rank_wa % V7X_LANES == 0 and rank_g % V7X_LANES == 0 and d_rwkv % V7X_MXU_EDGE == 0
    n_zr = 3 * d_rwkv + rank_wa + rank_g
    tile = lambda d: pl.BlockSpec((None, tm, d), lambda b, t: (b, t, 0))
    act = lambda d: jax.ShapeDtypeStruct((B, Lp, d), ACT_DTYPE)
    kern = functools.partial(_mixer_in_kernel, d_conv=d_conv, d_rwkv=d_rwkv, rank_w=rank_w,
                             rank_wa=rank_wa, rank_g=rank_g, hs=hs)
    return pl.pallas_call(
        kern,
        grid=(B, Lp // tm),
        in_specs=[tile(D), _resident(win.shape), _resident(tmix.shape), _resident(cw.shape),
                  _resident(cb.shape), _resident(clg.shape), _resident(clb.shape), _resident(cwo.shape),
                  _resident(wabd.shape), _resident(gup.shape), _resident(w0.shape), _resident(a0.shape),
                  _resident(kkw.shape), _resident(kaw.shape)],
        out_specs=[tile(D), tile(d_rwkv), tile(d_rwkv), tile(d_rwkv), tile(d_rwkv), tile(d_rwkv),
                   tile(d_rwkv), tile(d_rwkv), tile(D)],
        out_shape=[act(D), act(d_rwkv), jax.ShapeDtypeStruct((B, Lp, d_rwkv), F32), act(d_rwkv),
                   act(d_rwkv), act(d_rwkv), act(d_rwkv), act(d_rwkv), act(D)],
        scratch_shapes=[pltpu.VMEM((CONV_HALO + tm, d_conv), F32),
                        pltpu.VMEM((V7X_SUBLANES - 1, CONV_HALO + tm, CONV_LANES), F32),
                        pltpu.VMEM((tm, d_conv), F32),
                        pltpu.VMEM((8, n_zr), F32)],
        compiler_params=_params(("parallel", "arbitrary")),
        name="mixer_in",
    )(h3d, win, tmix, cw, cb, clg, clb, cwo, wabd, gup, w0, a0, kkw, kaw)


def _chunk_cumsum(x, C):
    pos = lax.broadcasted_iota(jnp.int32, x.shape, 0) % C
    sh = 1
    while sh < C:
        x = x + jnp.where(pos >= sh, pltpu.roll(x, sh, 0), 0.0)
        sh *= 2
    return x


def _rwkv_kernel(r_ref, lw_ref, k_ref, v_ref, kk_ref, b_ref, g_ref, rk_ref, lg_ref, lb_ref,
                 o_ref, s_ref, *, hs):
    TM, D = r_ref.shape
    C = RWKV_CHUNK
    Q = V7X_MXU_EDGE
    HPQ = Q // hs
    n_chunks = TM // C
    n_quads = D // Q

    @pl.when(pl.program_id(1) == 0)
    def _():
        s_ref[...] = jnp.zeros_like(s_ref)

    r = r_ref[...].astype(F32)
    k = k_ref[...].astype(F32)
    v = v_ref[...].astype(F32)
    kk = kk_ref[...].astype(F32)
    b = b_ref[...].astype(F32)
    lw = lw_ref[...]

    cum = _chunk_cumsum(lw, C)
    cum_last = jnp.concatenate(
        [jnp.broadcast_to(cum[(c + 1) * C - 1:(c + 1) * C, :], (C, D)) for c in range(n_chunks)], axis=0)
    e_in = jnp.exp(cum)
    e_neg = jnp.exp(-cum)
    e_rel = jnp.exp(cum_last - cum)
    a_t = _mx(-kk * jnp.exp(cum - lw))
    r_t = _mx(r * e_in)
    b_t = _mx(b * e_neg)
    k_t = _mx(k * e_neg)
    b_g = _mx(b * e_rel)
    k_g = _mx(k * e_rel)
    v_m = _mx(v)

    bd = _bd_mask(Q, hs)
    trow = lax.broadcasted_iota(jnp.int32, (C, Q), 0)
    spos = lax.broadcasted_iota(jnp.int32, (C, Q), 1) % C
    strict = spos < trow
    incl = spos <= trow
    eye_cat = (spos == trow).astype(F32)
    zero = jnp.zeros((), MXU_DTYPE)

    def bdt(x):
        return jnp.where(bd, jnp.concatenate([_mx(x)] * HPQ, axis=0), zero)

    units = [(c, q) for c in range(n_chunks) for q in range(n_quads)]
    rows = lambda c: slice(c * C, (c + 1) * C)
    cols = lambda q: slice(q * Q, (q + 1) * Q)

    lhs = {u: jnp.concatenate([a_t[rows(u[0]), cols(u[1])], r_t[rows(u[0]), cols(u[1])]], axis=0)
           for u in units}
    a_b = {u: _dot_nt(lhs[u], bdt(b_t[rows(u[0]), cols(u[1])])) for u in units}
    a_k = {u: _dot_nt(lhs[u], bdt(k_t[rows(u[0]), cols(u[1])])) for u in units}
    n_ab = {u: jnp.where(strict, a_b[u][:C], 0.0) for u in units}
    a_rb = {u: _mx(jnp.where(incl, a_b[u][C:], 0.0)) for u in units}
    both_mask = jnp.concatenate([strict, incl], axis=0)
    a_xk = {u: _mx(jnp.where(both_mask, a_k[u], 0.0)) for u in units}
    xkv = {u: _dot(a_xk[u], bdt(v_m[rows(u[0]), cols(u[1])])) for u in units}
    p = {u: eye_cat + n_ab[u] for u in units}
    npow = {u: _dot(_mx(n_ab[u]), bdt(n_ab[u])) for u in units}
    levels = C.bit_length() - 2
    for lvl in range(levels):
        if lvl + 1 < levels:
            both = {u: _dot(jnp.concatenate([_mx(p[u]), _mx(npow[u])], axis=0), bdt(npow[u])) for u in units}
            p = {u: p[u] + both[u][:C] for u in units}
            npow = {u: both[u][C:] for u in units}
        else:
            prod = {u: _dot(_mx(p[u]), bdt(npow[u])) for u in units}
            p = {u: p[u] + prod[u] for u in units}
    t_inv = {u: _mx(p[u]) for u in units}

    s_bd = [s_ref[q] for q in range(n_quads)]
    ys = []
    for c in range(n_chunks):
        qs = range(n_quads)
        sx = [_dot_nt(lhs[c, q], _mx(s_bd[q])) for q in qs]
        u_m = [_mx(_dot(t_inv[c, q], bdt(sx[q][:C] + xkv[c, q][:C]))) for q in qs]
        yu = [_dot(a_rb[c, q], bdt(u_m[q])) for q in qs]
        upd = [_dot_tn(jnp.concatenate([u_m[q], v_m[rows(c), cols(q)]], axis=0),
                       jnp.concatenate([b_g[rows(c), cols(q)], k_g[rows(c), cols(q)]], axis=0)) for q in qs]
        g_c = jnp.exp(cum[(c + 1) * C - 1:(c + 1) * C, :])
        s_bd = [s_bd[q] * g_c[:, cols(q)] + jnp.where(bd, upd[q], 0.0) for q in qs]
        ys.append(jnp.concatenate([sx[q][C:] + xkv[c, q][C:] + yu[q] for q in qs], axis=1))
    for q in range(n_quads):
        s_ref[q] = s_bd[q]
    y = jnp.concatenate(ys, axis=0)

    bd_ones = bd.astype(MXU_DTYPE)
    sums = _head_sum(jnp.concatenate([y, r * k * rk_ref[...]], axis=0), bd_ones)
    yc = y - sums[:TM] * (1.0 / hs)
    var = _head_sum(yc * yc, bd_ones) * (1.0 / hs)
    yn = yc * lax.rsqrt(var + GN_EPS) * lg_ref[...] + lb_ref[...]
    o_ref[...] = ((yn + sums[TM:] * v) * g_ref[...].astype(F32)).astype(o_ref.dtype)


def _rwkv_scan(r, lw, k, v, kk, b, g, rk, lg, lb, *, hs, tm):
    B, Lp, D = r.shape
    C = RWKV_CHUNK
    assert C == hs and V7X_MXU_EDGE % hs == 0 and D % V7X_MXU_EDGE == 0 and Lp % tm == 0 and tm % C == 0
    tile = pl.BlockSpec((None, tm, D), lambda bi, c: (bi, c, 0))
    return pl.pallas_call(
        functools.partial(_rwkv_kernel, hs=hs),
        grid=(B, Lp // tm),
        in_specs=[tile] * 7 + [_resident((1, D))] * 3,
        out_specs=tile,
        out_shape=jax.ShapeDtypeStruct((B, Lp, D), ACT_DTYPE),
        scratch_shapes=[pltpu.VMEM((D // V7X_MXU_EDGE, V7X_MXU_EDGE, V7X_MXU_EDGE), F32)],
        compiler_params=_params(("parallel", "arbitrary")),
        name="rwkv_scan",
    )(r, lw, k, v, kk, b, g, rk, lg, lb)


def _mixer_out_kernel(h_ref, yo_ref, pc_ref, sgr_ref, rwo_ref, wout_ref, g_ref, b_ref, o_ref, *, alpha):
    yr = _dot(_mx(yo_ref[...]), rwo_ref[...])
    hm = pc_ref[...].astype(F32) + sgr_ref[...].astype(F32) * yr
    m = _dot(_mx(hm), wout_ref[...])
    o_ref[...] = _layer_norm(alpha * h_ref[...] + m, g_ref[...], b_ref[...])


def _mixer_out(h2d, yo, pc, sgr, rwo, wout, g, b, *, alpha, tm):
    T, D = h2d.shape
    Dr = yo.shape[1]
    row = lambda d: pl.BlockSpec((tm, d), lambda i: (i, 0))
    return pl.pallas_call(
        functools.partial(_mixer_out_kernel, alpha=alpha),
        grid=(T // tm,),
        in_specs=[row(D), row(Dr), row(D), row(D), _resident(rwo.shape), _resident(wout.shape),
                  _resident((1, D)), _resident((1, D))],
        out_specs=row(D),
        out_shape=jax.ShapeDtypeStruct((T, D), F32),
        compiler_params=_params(("parallel",)),
        name="mixer_out",
    )(h2d, yo, pc, sgr, rwo, wout, g, b)


def kernel(x, meta_tokens, ln1_g, ln1_b, ffn1_wg, ffn1_wu, ffn1_wd, w_in, time_mix, conv_w, conv_b, conv_ln_g, conv_ln_b, conv_wo, w0, w_up, a0, a_up, g_up, k_k, k_a, r_k, lnx_g, lnx_b, rwkv_wo, w_out, ln2_g, ln2_b, ffn2_wg, ffn2_wu, ffn2_wd, ln3_g, ln3_b):
    B, L, D = x.shape
    depth = ffn1_wg.shape[0]
    n_meta = meta_tokens.shape[0]
    hs = r_k.shape[2]
    d_rwkv = w0.shape[1]
    rank_w = w_up.shape[1]
    alpha = (2 * depth) ** 0.25

    Lt = n_meta + L
    Lp = -(-Lt // (2 * RWKV_CHUNK)) * (2 * RWKV_CHUNK)
    tm_time = _largest_tile(Lp, 320, RWKV_CHUNK)
    tm_tok = _largest_tile(B * Lp, 512, 8)
    tm_rwkv = _largest_tile(Lp, RWKV_TILE, RWKV_CHUNK)

    meta = jnp.broadcast_to(meta_tokens.astype(x.dtype)[None], (B, n_meta, D))
    h = jnp.concatenate([meta, x, jnp.zeros((B, Lp - Lt, D), x.dtype)], axis=1)

    row = lambda p: p.reshape(1, -1).astype(F32)
    zeros_wa = lambda m: jnp.zeros_like(m)
    for l in range(depth):
        h = _ffn_ln(h.reshape(B * Lp, D), _mx(ffn1_wg[l]), _mx(ffn1_wu[l]), _mx(ffn1_wd[l]),
                    row(ln1_g[l]), row(ln1_b[l]), alpha=alpha, tm=tm_tok)
        wabd = jnp.concatenate(
            [jnp.concatenate([w_up[l], zeros_wa(w_up[l])], axis=1),
             jnp.concatenate([zeros_wa(a_up[l]), a_up[l]], axis=1)], axis=0)
        pc, r, lw, k, v, kk, b, g, sgr = _mixer_in(
            h.reshape(B, Lp, D), _mx(w_in[l]), row(time_mix[l]), conv_w[l].astype(F32), row(conv_b[l]),
            row(conv_ln_g[l]), row(conv_ln_b[l]), _mx(conv_wo[l]), _mx(wabd), _mx(g_up[l]),
            row(w0[l]), row(a0[l]), row(k_k[l]), row(k_a[l]), tm=tm_time, hs=hs, rank_w=rank_w)
        yo = _rwkv_scan(r, lw, k, v, kk, b, g, row(r_k[l]), row(lnx_g[l]), row(lnx_b[l]), hs=hs, tm=tm_rwkv)
        h = _mixer_out(h, yo.reshape(B * Lp, d_rwkv), pc.reshape(B * Lp, D), sgr.reshape(B * Lp, D),
                       _mx(rwkv_wo[l]), _mx(w_out[l]), row(ln2_g[l]), row(ln2_b[l]), alpha=alpha, tm=tm_tok)
        h = _ffn_ln(h, _mx(ffn2_wg[l]), _mx(ffn2_wu[l]), _mx(ffn2_wd[l]),
                    row(ln3_g[l]), row(ln3_b[l]), alpha=alpha, tm=tm_tok)
    return h.reshape(B, Lp, D)[:, n_meta:n_meta + L]
```

```python
import functools
import math

import jax
import jax.numpy as jnp
from jax import lax
from jax.experimental import pallas as pl
from jax.experimental.pallas import tpu as pltpu

LN_EPS = 1e-5
GN_EPS = 64e-5
KK_NORM_FLOOR = 1e-12
DECAY_SCALE = math.exp(-0.5)

F32 = jnp.float32
MXU_DTYPE = jnp.bfloat16
ACT_DTYPE = jnp.bfloat16

V7X_LANES = 128
V7X_SUBLANES = 8
V7X_MXU_EDGE = 256
V7X_VMEM_BYTES = 64 * 1024 * 1024
VMEM_LIMIT_BYTES = 56 * 1024 * 1024

RWKV_CHUNK = 64
RWKV_TILE = 320
CONV_HALO = 32
CONV_ROWS = 64
CONV_LANES = 256


def _dot(a, b):
    return jnp.dot(a, b, preferred_element_type=F32)


def _dot_nt(a, b):
    return lax.dot_general(a, b, (((1,), (1,)), ((), ())), preferred_element_type=F32)


def _dot_tn(a, b):
    return lax.dot_general(a, b, (((0,), (0,)), ((), ())), preferred_element_type=F32)


def _mx(x):
    return x.astype(MXU_DTYPE)


def _layer_norm(x, g, b, eps=LN_EPS):
    mu = jnp.mean(x, axis=-1, keepdims=True)
    xc = x - mu
    var = jnp.mean(xc * xc, axis=-1, keepdims=True)
    return xc * lax.rsqrt(var + eps) * g + b


def _largest_tile(n, cap, mult):
    best = None
    for t in range(mult, min(n, cap) + 1, mult):
        if n % t == 0:
            best = t
    assert best is not None, (n, cap, mult)
    return best


def _resident(shape):
    nd = len(shape)
    return pl.BlockSpec(shape, lambda *_: (0,) * nd, pipeline_mode=pl.Buffered(1))


def _params(sem):
    return pltpu.CompilerParams(dimension_semantics=sem, vmem_limit_bytes=VMEM_LIMIT_BYTES)


def _ffn_kernel(h_ref, wg_ref, wu_ref, wd_ref, g_ref, b_ref, o_ref, *, alpha):
    x = h_ref[...]
    xb = _mx(x)
    gate = _dot(xb, wg_ref[...])
    up = _dot(xb, wu_ref[...])
    act = _mx(gate * jax.nn.sigmoid(gate) * up)
    y = _dot(act, wd_ref[...])
    o_ref[...] = _layer_norm(alpha * x + 0.5 * y, g_ref[...], b_ref[...])


def _ffn_ln(h2d, wg, wu, wd, g, b, *, alpha, tm):
    T, D = h2d.shape
    F = wg.shape[1]
    row = pl.BlockSpec((tm, D), lambda i: (i, 0))
    return pl.pallas_call(
        functools.partial(_ffn_kernel, alpha=alpha),
        grid=(T // tm,),
        in_specs=[row, _resident((D, F)), _resident((D, F)), _resident((F, D)),
                  _resident((1, D)), _resident((1, D))],
        out_specs=row,
        out_shape=jax.ShapeDtypeStruct((T, D), F32),
        compiler_params=_params(("parallel",)),
        name="ffn_ln",
    )(h2d, wg, wu, wd, g, b)


def _head_sum(x, bd_ones):
    Q = bd_ones.shape[0]
    outs = [_dot(_mx(x[:, q * Q:(q + 1) * Q]), bd_ones) for q in range(x.shape[1] // Q)]
    return outs[0] if len(outs) == 1 else jnp.concatenate(outs, axis=1)


def _bd_mask(Q, hs):
    row = lax.broadcasted_iota(jnp.int32, (Q, Q), 0)
    col = lax.broadcasted_iota(jnp.int32, (Q, Q), 1)
    return (row // hs) == (col // hs)


def _mixer_in_kernel(h_ref, win_ref, tmix_ref, cw_ref, cb_ref, clg_ref, clb_ref, cwo_ref,
                     wabd_ref, gup_ref, w0_ref, a0_ref, kkw_ref, kaw_ref,
                     pc_ref, r_ref, lw_ref, k_ref, v_ref, kk_ref, b_ref, g_ref, sgr_ref,
                     ubuf, shbuf, cbuf, zprev, *, d_conv, d_rwkv, rank_w, rank_wa, rank_g, hs):
    TM = h_ref.shape[0]
    W = cw_ref.shape[0]
    t = pl.program_id(1)

    @pl.when(t == 0)
    def _():
        ubuf[0:CONV_HALO, :] = jnp.zeros((CONV_HALO, d_conv), F32)
        zprev[...] = jnp.zeros_like(zprev)

    hb = _mx(h_ref[...])
    c0 = 2 * d_conv
    c1 = c0 + 3 * d_rwkv + rank_wa + rank_g
    c2 = c1 + d_conv

    ab = _dot(hb, win_ref[:, 0:c0])
    ubuf[CONV_HALO:CONV_HALO + TM, :] = ab[:, :d_conv] * jax.nn.sigmoid(ab[:, d_conv:])
    base = CONV_HALO - (W - 1)
    n_rows = CONV_HALO + TM
    for g in range(d_conv // CONV_LANES):
        ln = slice(g * CONV_LANES, (g + 1) * CONV_LANES)
        window = ubuf[:, ln]
        for s in range(1, V7X_SUBLANES):
            shbuf[s - 1] = pltpu.roll(window, n_rows - s, 0)
        for i in range(TM // CONV_ROWS):
            acc = None
            for j in range(W):
                s = (base + j) % V7X_SUBLANES
                o = base + j - s + i * CONV_ROWS
                rows_j = ubuf[o:o + CONV_ROWS, ln] if s == 0 else shbuf[s - 1, o:o + CONV_ROWS, :]
                term = rows_j * cw_ref[j:j + 1, ln]
                acc = term if acc is None else acc + term
            cbuf[i * CONV_ROWS:(i + 1) * CONV_ROWS, ln] = acc
    ubuf[0:CONV_HALO, :] = ubuf[TM:TM + CONV_HALO, :]
    cn = _layer_norm(cbuf[...] + cb_ref[...], clg_ref[...], clb_ref[...])
    cs = _mx(cn * jax.nn.sigmoid(cn))
    yc = _dot(cs, cwo_ref[...])
    gate_c = _dot(hb, win_ref[:, c1:c2])
    pc_ref[...] = (jax.nn.sigmoid(gate_c) * yc).astype(pc_ref.dtype)
    gate_r = _dot(hb, win_ref[:, c2:c2 + d_conv])
    sgr_ref[...] = jax.nn.sigmoid(gate_r).astype(sgr_ref.dtype)

    zr = _dot(hb, win_ref[:, c0:c1])
    rows = lax.broadcasted_iota(jnp.int32, zr.shape, 0)
    prev = jnp.where(rows == 0, zprev[7:8, :], pltpu.roll(zr, 1, 0))
    zprev[...] = zr[TM - 8:TM, :]
    zr = zr + (prev - zr) * tmix_ref[...]
    r = zr[:, 0:d_rwkv]
    k = zr[:, d_rwkv:2 * d_rwkv]
    v = zr[:, 2 * d_rwkv:3 * d_rwkv]
    zwa = zr[:, 3 * d_rwkv:3 * d_rwkv + rank_wa]
    zg = zr[:, 3 * d_rwkv + rank_wa:3 * d_rwkv + rank_wa + rank_g]
    lanes = lax.broadcasted_iota(jnp.int32, zwa.shape, 1)
    xwa = jnp.where(lanes < rank_w, jnp.tanh(zwa), zwa)
    lora = _dot(_mx(xwa), wabd_ref[...])
    g = _dot(_mx(jax.nn.sigmoid(zg)), gup_ref[...])
    lw_ref[...] = -DECAY_SCALE * jax.nn.sigmoid(w0_ref[...] + lora[:, :d_rwkv])
    a = jax.nn.sigmoid(a0_ref[...] + lora[:, d_rwkv:])
    kk = k * kkw_ref[...]
    bd_ones = _bd_mask(V7X_MXU_EDGE, hs).astype(MXU_DTYPE)
    kk = kk * lax.rsqrt(jnp.maximum(_head_sum(kk * kk, bd_ones), KK_NORM_FLOOR * KK_NORM_FLOOR))
    r_ref[...] = r.astype(r_ref.dtype)
    k_ref[...] = (k * (1.0 + (a - 1.0) * kaw_ref[...])).astype(k_ref.dtype)
    v_ref[...] = v.astype(v_ref.dtype)
    kk_ref[...] = kk.astype(kk_ref.dtype)
    b_ref[...] = (kk * a).astype(b_ref.dtype)
    g_ref[...] = g.astype(g_ref.dtype)


def _mixer_in(h3d, win, tmix, cw, cb, clg, clb, cwo, wabd, gup, w0, a0, kkw, kaw, *, tm, hs, rank_w):
    B, Lp, D = h3d.shape
    d_conv = cwo.shape[0]
    d_rwkv = w0.shape[1]
    rank_wa = wabd.shape[0]
    rank_g = gup.shape[0]
    W = cw.shape[0]
    assert W - 1 <= CONV_HALO and tm % CONV_ROWS == 0 and tm >= CONV_HALO and d_conv % CONV_LANES == 0
    assert rank_wa % V7X_LANES == 0 and rank_g % V7X_LANES == 0 and d_rwkv % V7X_MXU_EDGE == 0
    n_zr = 3 * d_rwkv + rank_wa + rank_g
    tile = lambda d: pl.BlockSpec((None, tm, d), lambda b, t: (b, t, 0))
    act = lambda d: jax.ShapeDtypeStruct((B, Lp, d), ACT_DTYPE)
    kern = functools.partial(_mixer_in_kernel, d_conv=d_conv, d_rwkv=d_rwkv, rank_w=rank_w,
                             rank_wa=rank_wa, rank_g=rank_g, hs=hs)
    return pl.pallas_call(
        kern,
        grid=(B, Lp // tm),
        in_specs=[tile(D), _resident(win.shape), _resident(tmix.shape), _resident(cw.shape),
                  _resident(cb.shape), _resident(clg.shape), _resident(clb.shape), _resident(cwo.shape),
                  _resident(wabd.shape), _resident(gup.shape), _resident(w0.shape), _resident(a0.shape),
                  _resident(kkw.shape), _resident(kaw.shape)],
        out_specs=[tile(D), tile(d_rwkv), tile(d_rwkv), tile(d_rwkv), tile(d_rwkv), tile(d_rwkv),
                   tile(d_rwkv), tile(d_rwkv), tile(D)],
        out_shape=[act(D), act(d_rwkv), jax.ShapeDtypeStruct((B, Lp, d_rwkv), F32), act(d_rwkv),
                   act(d_rwkv), act(d_rwkv), act(d_rwkv), act(d_rwkv), act(D)],
        scratch_shapes=[pltpu.VMEM((CONV_HALO + tm, d_conv), F32),
                        pltpu.VMEM((V7X_SUBLANES - 1, CONV_HALO + tm, CONV_LANES), F32),
                        pltpu.VMEM((tm, d_conv), F32),
                        pltpu.VMEM((8, n_zr), F32)],
        compiler_params=_params(("parallel", "arbitrary")),
        name="mixer_in",
    )(h3d, win, tmix, cw, cb, clg, clb, cwo, wabd, gup, w0, a0, kkw, kaw)


def _chunk_cumsum(x, C):
    pos = lax.broadcasted_iota(jnp.int32, x.shape, 0) % C
    sh = 1
    while sh < C:
        x = x + jnp.where(pos >= sh, pltpu.roll(x, sh, 0), 0.0)
        sh *= 2
    return x


def _rwkv_kernel(r_ref, lw_ref, k_ref, v_ref, kk_ref, b_ref, g_ref, rk_ref, lg_ref, lb_ref,
                 o_ref, s_ref, *, hs):
    TM, D = r_ref.shape
    C = RWKV_CHUNK
    Q = V7X_MXU_EDGE
    HPQ = Q // hs
    n_chunks = TM // C
    n_quads = D // Q

    @pl.when(pl.program_id(1) == 0)
    def _():
        s_ref[...] = jnp.zeros_like(s_ref)

    bd = _bd_mask(Q, hs)
    trow = lax.broadcasted_iota(jnp.int32, (C, Q), 0)
    spos = lax.broadcasted_iota(jnp.int32, (C, Q), 1) % C
    strict = spos < trow
    incl = spos <= trow
    eye_cat = (spos == trow).astype(F32)
    zero = jnp.zeros((), MXU_DTYPE)

    def bdt(x):
        return jnp.where(bd, jnp.concatenate([_mx(x)] * HPQ, axis=0), zero)

    units = [(c, q) for c in range(n_chunks) for q in range(n_quads)]
    rows = lambda c: slice(c * C, (c + 1) * C)
    cols = lambda q: slice(q * Q, (q + 1) * Q)

    def scaled_operands(c):
        rc = r_ref[rows(c), :].astype(F32)
        kc = k_ref[rows(c), :].astype(F32)
        kkc = kk_ref[rows(c), :].astype(F32)
        bc = b_ref[rows(c), :].astype(F32)
        lw = lw_ref[rows(c), :]
        cum = _chunk_cumsum(lw, C)
        cum_last = cum[C - 1:C, :]
        e_neg = jnp.exp(-cum)
        e_rel = jnp.exp(cum_last - cum)
        return dict(a_t=_mx(-kkc * jnp.exp(cum - lw)), r_t=_mx(rc * jnp.exp(cum)), b_t=_mx(bc * e_neg),
                    k_t=_mx(kc * e_neg), b_g=_mx(bc * e_rel), k_g=_mx(kc * e_rel),
                    v_m=_mx(v_ref[rows(c), :]), g_c=jnp.exp(cum_last))

    ops, lhs, a_b, a_k = {}, {}, {}, {}
    for c in range(n_chunks):
        ops[c] = scaled_operands(c)
        for q in range(n_quads):
            u = (c, q)
            lhs[u] = jnp.concatenate([ops[c]["a_t"][:, cols(q)], ops[c]["r_t"][:, cols(q)]], axis=0)
            a_b[u] = _dot_nt(lhs[u], bdt(ops[c]["b_t"][:, cols(q)]))
            a_k[u] = _dot_nt(lhs[u], bdt(ops[c]["k_t"][:, cols(q)]))
    n_ab = {u: jnp.where(strict, a_b[u][:C], 0.0) for u in units}
    a_rb = {u: _mx(jnp.where(incl, a_b[u][C:], 0.0)) for u in units}
    both_mask = jnp.concatenate([strict, incl], axis=0)
    a_xk = {u: _mx(jnp.where(both_mask, a_k[u], 0.0)) for u in units}
    xkv = {u: _dot(a_xk[u], bdt(ops[u[0]]["v_m"][:, cols(u[1])])) for u in units}
    p = {u: eye_cat + n_ab[u] for u in units}
    npow = {u: _dot(_mx(n_ab[u]), bdt(n_ab[u])) for u in units}
    levels = C.bit_length() - 2
    for lvl in range(levels):
        if lvl + 1 < levels:
            both = {u: _dot(jnp.concatenate([_mx(p[u]), _mx(npow[u])], axis=0), bdt(npow[u])) for u in units}
            p = {u: p[u] + both[u][:C] for u in units}
            npow = {u: both[u][C:] for u in units}
        else:
            prod = {u: _dot(_mx(p[u]), bdt(npow[u])) for u in units}
            p = {u: p[u] + prod[u] for u in units}
    t_inv = {u: _mx(p[u]) for u in units}

    s_bd = [s_ref[q] for q in range(n_quads)]
    ys = []
    for c in range(n_chunks):
        qs = range(n_quads)
        sx = [_dot_nt(lhs[c, q], _mx(s_bd[q])) for q in qs]
        u_m = [_mx(_dot(t_inv[c, q], bdt(sx[q][:C] + xkv[c, q][:C]))) for q in qs]
        yu = [_dot(a_rb[c, q], bdt(u_m[q])) for q in qs]
        oc = ops[c]
        upd = [_dot_tn(jnp.concatenate([u_m[q], oc["v_m"][:, cols(q)]], axis=0),
                       jnp.concatenate([oc["b_g"][:, cols(q)], oc["k_g"][:, cols(q)]], axis=0)) for q in qs]
        s_bd = [s_bd[q] * oc["g_c"][:, cols(q)] + jnp.where(bd, upd[q], 0.0) for q in qs]
        ys.append(jnp.concatenate([sx[q][C:] + xkv[c, q][C:] + yu[q] for q in qs], axis=1))
    for q in range(n_quads):
        s_ref[q] = s_bd[q]
    y = jnp.concatenate(ys, axis=0)

    bd_ones = bd.astype(MXU_DTYPE)
    rk = r_ref[...].astype(F32) * k_ref[...].astype(F32) * rk_ref[...]
    sums = _head_sum(jnp.concatenate([y, rk], axis=0), bd_ones)
    yc = y - sums[:TM] * (1.0 / hs)
    var = _head_sum(yc * yc, bd_ones) * (1.0 / hs)
    yn = yc * lax.rsqrt(var + GN_EPS) * lg_ref[...] + lb_ref[...]
    o_ref[...] = ((yn + sums[TM:] * v_ref[...].astype(F32)) * g_ref[...].astype(F32)).astype(o_ref.dtype)


def _rwkv_scan(r, lw, k, v, kk, b, g, rk, lg, lb, *, hs, tm):
    B, Lp, D = r.shape
    C = RWKV_CHUNK
    assert C == hs and V7X_MXU_EDGE % hs == 0 and D % V7X_MXU_EDGE == 0 and Lp % tm == 0 and tm % C == 0
    tile = pl.BlockSpec((None, tm, D), lambda bi, c: (bi, c, 0))
    return pl.pallas_call(
        functools.partial(_rwkv_kernel, hs=hs),
        grid=(B, Lp // tm),
        in_specs=[tile] * 7 + [_resident((1, D))] * 3,
        out_specs=tile,
        out_shape=jax.ShapeDtypeStruct((B, Lp, D), ACT_DTYPE),
        scratch_shapes=[pltpu.VMEM((D // V7X_MXU_EDGE, V7X_MXU_EDGE, V7X_MXU_EDGE), F32)],
        compiler_params=_params(("parallel", "arbitrary")),
        name="rwkv_scan",
    )(r, lw, k, v, kk, b, g, rk, lg, lb)


def _mixer_out_kernel(h_ref, yo_ref, pc_ref, sgr_ref, rwo_ref, wout_ref, g_ref, b_ref, o_ref, *, alpha):
    yr = _dot(_mx(yo_ref[...]), rwo_ref[...])
    hm = pc_ref[...].astype(F32) + sgr_ref[...].astype(F32) * yr
    m = _dot(_mx(hm), wout_ref[...])
    o_ref[...] = _layer_norm(alpha * h_ref[...] + m, g_ref[...], b_ref[...])


def _mixer_out(h2d, yo, pc, sgr, rwo, wout, g, b, *, alpha, tm):
    T, D = h2d.shape
    Dr = yo.shape[1]
    row = lambda d: pl.BlockSpec((tm, d), lambda i: (i, 0))
    return pl.pallas_call(
        functools.partial(_mixer_out_kernel, alpha=alpha),
        grid=(T // tm,),
        in_specs=[row(D), row(Dr), row(D), row(D), _resident(rwo.shape), _resident(wout.shape),
                  _resident((1, D)), _resident((1, D))],
        out_specs=row(D),
        out_shape=jax.ShapeDtypeStruct((T, D), F32),
        compiler_params=_params(("parallel",)),
        name="mixer_out",
    )(h2d, yo, pc, sgr, rwo, wout, g, b)


def kernel(x, meta_tokens, ln1_g, ln1_b, ffn1_wg, ffn1_wu, ffn1_wd, w_in, time_mix, conv_w, conv_b, conv_ln_g, conv_ln_b, conv_wo, w0, w_up, a0, a_up, g_up, k_k, k_a, r_k, lnx_g, lnx_b, rwkv_wo, w_out, ln2_g, ln2_b, ffn2_wg, ffn2_wu, ffn2_wd, ln3_g, ln3_b):
    B, L, D = x.shape
    depth = ffn1_wg.shape[0]
    n_meta = meta_tokens.shape[0]
    hs = r_k.shape[2]
    d_rwkv = w0.shape[1]
    rank_w = w_up.shape[1]
    alpha = (2 * depth) ** 0.25

    Lt = n_meta + L
    Lp = -(-Lt // (2 * RWKV_CHUNK)) * (2 * RWKV_CHUNK)
    tm_time = _largest_tile(Lp, 320, RWKV_CHUNK)
    tm_tok = _largest_tile(B * Lp, 512, 8)
    tm_rwkv = _largest_tile(Lp, RWKV_TILE, RWKV_CHUNK)

    meta = jnp.broadcast_to(meta_tokens.astype(x.dtype)[None], (B, n_meta, D))
    h = jnp.concatenate([meta, x, jnp.zeros((B, Lp - Lt, D), x.dtype)], axis=1)

    row = lambda p: p.reshape(1, -1).astype(F32)
    zeros_wa = lambda m: jnp.zeros_like(m)
    for l in range(depth):
        h = _ffn_ln(h.reshape(B * Lp, D), _mx(ffn1_wg[l]), _mx(ffn1_wu[l]), _mx(ffn1_wd[l]),
                    row(ln1_g[l]), row(ln1_b[l]), alpha=alpha, tm=tm_tok)
        wabd = jnp.concatenate(
            [jnp.concatenate([w_up[l], zeros_wa(w_up[l])], axis=1),
             jnp.concatenate([zeros_wa(a_up[l]), a_up[l]], axis=1)], axis=0)
        pc, r, lw, k, v, kk, b, g, sgr = _mixer_in(
            h.reshape(B, Lp, D), _mx(w_in[l]), row(time_mix[l]), conv_w[l].astype(F32), row(conv_b[l]),
            row(conv_ln_g[l]), row(conv_ln_b[l]), _mx(conv_wo[l]), _mx(wabd), _mx(g_up[l]),
            row(w0[l]), row(a0[l]), row(k_k[l]), row(k_a[l]), tm=tm_time, hs=hs, rank_w=rank_w)
        yo = _rwkv_scan(r, lw, k, v, kk, b, g, row(r_k[l]), row(lnx_g[l]), row(lnx_b[l]), hs=hs, tm=tm_rwkv)
        h = _mixer_out(h, yo.reshape(B * Lp, d_rwkv), pc.reshape(B * Lp, D), sgr.reshape(B * Lp, D),
                       _mx(rwkv_wo[l]), _mx(w_out[l]), row(ln2_g[l]), row(ln2_b[l]), alpha=alpha, tm=tm_tok)
        h = _ffn_ln(h, _mx(ffn2_wg[l]), _mx(ffn2_wu[l]), _mx(ffn2_wd[l]),
                    row(ln3_g[l]), row(ln3_b[l]), alpha=alpha, tm=tm_tok)
    return h.reshape(B, Lp, D)[:, n_meta:n_meta + L]
```

```python
import functools
import math

import jax
import jax.numpy as jnp
from jax import lax
from jax.experimental import pallas as pl
from jax.experimental.pallas import tpu as pltpu

LN_EPS = 1e-5
GN_EPS = 64e-5
KK_NORM_FLOOR = 1e-12
DECAY_SCALE = math.exp(-0.5)

F32 = jnp.float32
MXU_DTYPE = jnp.bfloat16
ACT_DTYPE = jnp.bfloat16

V7X_LANES = 128
V7X_SUBLANES = 8
V7X_MXU_EDGE = 256
V7X_VMEM_BYTES = 64 * 1024 * 1024
VMEM_LIMIT_BYTES = 56 * 1024 * 1024
FUSED_VMEM_LIMIT_BYTES = 60 * 1024 * 1024

RWKV_CHUNK = 64
RWKV_TILE = 320
CONV_HALO = 32
CONV_ROWS = 64
CONV_LANES = 256
FFN_CHUNK = 256


def _dot(a, b):
    return jnp.dot(a, b, preferred_element_type=F32)


def _dot_nt(a, b):
    return lax.dot_general(a, b, (((1,), (1,)), ((), ())), preferred_element_type=F32)


def _dot_tn(a, b):
    return lax.dot_general(a, b, (((0,), (0,)), ((), ())), preferred_element_type=F32)


def _mx(x):
    return x.astype(MXU_DTYPE)


def _layer_norm(x, g, b, eps=LN_EPS):
    mu = jnp.mean(x, axis=-1, keepdims=True)
    xc = x - mu
    var = jnp.mean(xc * xc, axis=-1, keepdims=True)
    return xc * lax.rsqrt(var + eps) * g + b


def _largest_tile(n, cap, mult):
    best = None
    for t in range(mult, min(n, cap) + 1, mult):
        if n % t == 0:
            best = t
    assert best is not None, (n, cap, mult)
    return best


def _resident(shape):
    nd = len(shape)
    return pl.BlockSpec(shape, lambda *_: (0,) * nd, pipeline_mode=pl.Buffered(1))


def _params(sem):
    return pltpu.CompilerParams(dimension_semantics=sem, vmem_limit_bytes=VMEM_LIMIT_BYTES)


def _ffn_kernel(h_ref, wg_ref, wu_ref, wd_ref, g_ref, b_ref, o_ref, *, alpha):
    x = h_ref[...]
    xb = _mx(x)
    gate = _dot(xb, wg_ref[...])
    up = _dot(xb, wu_ref[...])
    act = _mx(gate * jax.nn.sigmoid(gate) * up)
    y = _dot(act, wd_ref[...])
    o_ref[...] = _layer_norm(alpha * x + 0.5 * y, g_ref[...], b_ref[...])


def _ffn_ln(h2d, wg, wu, wd, g, b, *, alpha, tm):
    T, D = h2d.shape
    F = wg.shape[1]
    row = pl.BlockSpec((tm, D), lambda i: (i, 0))
    return pl.pallas_call(
        functools.partial(_ffn_kernel, alpha=alpha),
        grid=(T // tm,),
        in_specs=[row, _resident((D, F)), _resident((D, F)), _resident((F, D)),
                  _resident((1, D)), _resident((1, D))],
        out_specs=row,
        out_shape=jax.ShapeDtypeStruct((T, D), F32),
        compiler_params=_params(("parallel",)),
        name="ffn_ln",
    )(h2d, wg, wu, wd, g, b)


def _head_sum(x, bd_ones):
    Q = bd_ones.shape[0]
    outs = [_dot(_mx(x[:, q * Q:(q + 1) * Q]), bd_ones) for q in range(x.shape[1] // Q)]
    return outs[0] if len(outs) == 1 else jnp.concatenate(outs, axis=1)


def _bd_mask(Q, hs):
    row = lax.broadcasted_iota(jnp.int32, (Q, Q), 0)
    col = lax.broadcasted_iota(jnp.int32, (Q, Q), 1)
    return (row // hs) == (col // hs)


def _mixer_in_kernel(h_ref, win_ref, tmix_ref, wabd_ref, gup_ref, w0_ref, a0_ref, kkw_ref, kaw_ref,
                     u_ref, sgc_ref, r_ref, lw_ref, k_ref, v_ref, kk_ref, b_ref, g_ref, sgr_ref,
                     zprev, *, d_conv, d_rwkv, rank_w, rank_wa, rank_g, hs):
    TM, D = h_ref.shape

    @pl.when(pl.program_id(1) == 0)
    def _():
        zprev[...] = jnp.zeros_like(zprev)

    hb = _mx(h_ref[...])
    c0 = 2 * d_conv
    c1 = c0 + 3 * d_rwkv + rank_wa + rank_g

    ab = _dot(hb, win_ref[:, 0:c0])
    u_ref[...] = (ab[:, :d_conv] * jax.nn.sigmoid(ab[:, d_conv:])).astype(u_ref.dtype)
    sgc_ref[...] = jax.nn.sigmoid(_dot(hb, win_ref[:, c1:c1 + D])).astype(sgc_ref.dtype)
    sgr_ref[...] = jax.nn.sigmoid(_dot(hb, win_ref[:, c1 + D:c1 + 2 * D])).astype(sgr_ref.dtype)

    zr = _dot(hb, win_ref[:, c0:c1])
    rows = lax.broadcasted_iota(jnp.int32, zr.shape, 0)
    prev = jnp.where(rows == 0, zprev[V7X_SUBLANES - 1:V7X_SUBLANES, :], pltpu.roll(zr, 1, 0))
    zprev[...] = zr[TM - V7X_SUBLANES:TM, :]
    zr = zr + (prev - zr) * tmix_ref[...]
    r = zr[:, 0:d_rwkv]
    k = zr[:, d_rwkv:2 * d_rwkv]
    v = zr[:, 2 * d_rwkv:3 * d_rwkv]
    zwa = zr[:, 3 * d_rwkv:3 * d_rwkv + rank_wa]
    zg = zr[:, 3 * d_rwkv + rank_wa:3 * d_rwkv + rank_wa + rank_g]
    lanes = lax.broadcasted_iota(jnp.int32, zwa.shape, 1)
    xwa = jnp.where(lanes < rank_w, jnp.tanh(zwa), zwa)
    lora = _dot(_mx(xwa), wabd_ref[...])
    g = _dot(_mx(jax.nn.sigmoid(zg)), gup_ref[...])
    lw_ref[...] = -DECAY_SCALE * jax.nn.sigmoid(w0_ref[...] + lora[:, :d_rwkv])
    a = jax.nn.sigmoid(a0_ref[...] + lora[:, d_rwkv:])
    kk = k * kkw_ref[...]
    bd_ones = _bd_mask(V7X_MXU_EDGE, hs).astype(MXU_DTYPE)
    kk = kk * lax.rsqrt(jnp.maximum(_head_sum(kk * kk, bd_ones), KK_NORM_FLOOR * KK_NORM_FLOOR))
    r_ref[...] = r.astype(r_ref.dtype)
    k_ref[...] = (k * (1.0 + (a - 1.0) * kaw_ref[...])).astype(k_ref.dtype)
    v_ref[...] = v.astype(v_ref.dtype)
    kk_ref[...] = kk.astype(kk_ref.dtype)
    b_ref[...] = (kk * a).astype(b_ref.dtype)
    g_ref[...] = g.astype(g_ref.dtype)


def _mixer_in(h3d, win, tmix, wabd, gup, w0, a0, kkw, kaw, *, tm, hs, rank_w):
    B, Lp, D = h3d.shape
    d_rwkv = w0.shape[1]
    rank_wa = wabd.shape[0]
    rank_g = gup.shape[0]
    n_zr = 3 * d_rwkv + rank_wa + rank_g
    d_conv = (win.shape[1] - n_zr - 2 * D) // 2
    assert 2 * d_conv + n_zr + 2 * D == win.shape[1] and d_conv % V7X_LANES == 0
    assert rank_wa % V7X_LANES == 0 and rank_g % V7X_LANES == 0 and d_rwkv % V7X_MXU_EDGE == 0
    tile = lambda d: pl.BlockSpec((None, tm, d), lambda b, t: (b, t, 0))
    act = lambda d: jax.ShapeDtypeStruct((B, Lp, d), ACT_DTYPE)
    kern = functools.partial(_mixer_in_kernel, d_conv=d_conv, d_rwkv=d_rwkv, rank_w=rank_w,
                             rank_wa=rank_wa, rank_g=rank_g, hs=hs)
    return pl.pallas_call(
        kern,
        grid=(B, Lp // tm),
        in_specs=[tile(D), _resident(win.shape), _resident(tmix.shape),
                  _resident(wabd.shape), _resident(gup.shape), _resident(w0.shape), _resident(a0.shape),
                  _resident(kkw.shape), _resident(kaw.shape)],
        out_specs=[tile(d_conv), tile(D), tile(d_rwkv), tile(d_rwkv), tile(d_rwkv), tile(d_rwkv), tile(d_rwkv),
                   tile(d_rwkv), tile(d_rwkv), tile(D)],
        out_shape=[act(d_conv), act(D), act(d_rwkv), jax.ShapeDtypeStruct((B, Lp, d_rwkv), F32), act(d_rwkv),
                   act(d_rwkv), act(d_rwkv), act(d_rwkv), act(d_rwkv), act(D)],
        scratch_shapes=[pltpu.VMEM((V7X_SUBLANES, n_zr), F32)],
        compiler_params=_params(("parallel", "arbitrary")),
        name="mixer_in",
    )(h3d, win, tmix, wabd, gup, w0, a0, kkw, kaw)


def _chunk_cumsum(x, C):
    pos = lax.broadcasted_iota(jnp.int32, x.shape, 0) % C
    sh = 1
    while sh < C:
        x = x + jnp.where(pos >= sh, pltpu.roll(x, sh, 0), 0.0)
        sh *= 2
    return x


def _rwkv_kernel(r_ref, lw_ref, k_ref, v_ref, kk_ref, b_ref, g_ref, rk_ref, lg_ref, lb_ref,
                 o_ref, s_ref, *, hs):
    TM, D = r_ref.shape
    C = RWKV_CHUNK
    Q = V7X_MXU_EDGE
    HPQ = Q // hs
    n_chunks = TM // C
    n_quads = D // Q

    @pl.when(pl.program_id(1) == 0)
    def _():
        s_ref[...] = jnp.zeros_like(s_ref)

    bd = _bd_mask(Q, hs)
    trow = lax.broadcasted_iota(jnp.int32, (C, Q), 0)
    spos = lax.broadcasted_iota(jnp.int32, (C, Q), 1) % C
    strict = spos < trow
    incl = spos <= trow
    eye_cat = (spos == trow).astype(F32)
    zero = jnp.zeros((), MXU_DTYPE)

    def bdt(x):
        return jnp.where(bd, jnp.concatenate([_mx(x)] * HPQ, axis=0), zero)

    units = [(c, q) for c in range(n_chunks) for q in range(n_quads)]
    rows = lambda c: slice(c * C, (c + 1) * C)
    cols = lambda q: slice(q * Q, (q + 1) * Q)

    def scaled_operands(c):
        rc = r_ref[rows(c), :].astype(F32)
        kc = k_ref[rows(c), :].astype(F32)
        kkc = kk_ref[rows(c), :].astype(F32)
        bc = b_ref[rows(c), :].astype(F32)
        lw = lw_ref[rows(c), :]
        cum = _chunk_cumsum(lw, C)
        cum_last = cum[C - 1:C, :]
        e_neg = jnp.exp(-cum)
        e_rel = jnp.exp(cum_last - cum)
        return dict(a_t=_mx(-kkc * jnp.exp(cum - lw)), r_t=_mx(rc * jnp.exp(cum)), b_t=_mx(bc * e_neg),
                    k_t=_mx(kc * e_neg), b_g=_mx(bc * e_rel), k_g=_mx(kc * e_rel),
                    v_m=_mx(v_ref[rows(c), :]), g_c=jnp.exp(cum_last))

    ops, lhs, a_b, a_k = {}, {}, {}, {}
    for c in range(n_chunks):
        ops[c] = scaled_operands(c)
        for q in range(n_quads):
            u = (c, q)
            lhs[u] = jnp.concatenate([ops[c]["a_t"][:, cols(q)], ops[c]["r_t"][:, cols(q)]], axis=0)
            a_b[u] = _dot_nt(lhs[u], bdt(ops[c]["b_t"][:, cols(q)]))
            a_k[u] = _dot_nt(lhs[u], bdt(ops[c]["k_t"][:, cols(q)]))
    n_ab = {u: jnp.where(strict, a_b[u][:C], 0.0) for u in units}
    a_rb = {u: _mx(jnp.where(incl, a_b[u][C:], 0.0)) for u in units}
    both_mask = jnp.concatenate([strict, incl], axis=0)
    a_xk = {u: _mx(jnp.where(both_mask, a_k[u], 0.0)) for u in units}
    xkv = {u: _dot(a_xk[u], bdt(ops[u[0]]["v_m"][:, cols(u[1])])) for u in units}
    p = {u: eye_cat + n_ab[u] for u in units}
    npow = {u: _dot(_mx(n_ab[u]), bdt(n_ab[u])) for u in units}
    levels = C.bit_length() - 2
    for lvl in range(levels):
        if lvl + 1 < levels:
            both = {u: _dot(jnp.concatenate([_mx(p[u]), _mx(npow[u])], axis=0), bdt(npow[u])) for u in units}
            p = {u: p[u] + both[u][:C] for u in units}
            npow = {u: both[u][C:] for u in units}
        else:
            prod = {u: _dot(_mx(p[u]), bdt(npow[u])) for u in units}
            p = {u: p[u] + prod[u] for u in units}
    t_inv = {u: _mx(p[u]) for u in units}

    s_bd = [s_ref[q] for q in range(n_quads)]
    ys = []
    for c in range(n_chunks):
        qs = range(n_quads)
        sx = [_dot_nt(lhs[c, q], _mx(s_bd[q])) for q in qs]
        u_m = [_mx(_dot(t_inv[c, q], bdt(sx[q][:C] + xkv[c, q][:C]))) for q in qs]
        yu = [_dot(a_rb[c, q], bdt(u_m[q])) for q in qs]
        oc = ops[c]
        upd = [_dot_tn(jnp.concatenate([u_m[q], oc["v_m"][:, cols(q)]], axis=0),
                       jnp.concatenate([oc["b_g"][:, cols(q)], oc["k_g"][:, cols(q)]], axis=0)) for q in qs]
        s_bd = [s_bd[q] * oc["g_c"][:, cols(q)] + jnp.where(bd, upd[q], 0.0) for q in qs]
        ys.append(jnp.concatenate([sx[q][C:] + xkv[c, q][C:] + yu[q] for q in qs], axis=1))
    for q in range(n_quads):
        s_ref[q] = s_bd[q]
    y = jnp.concatenate(ys, axis=0)

    bd_ones = bd.astype(MXU_DTYPE)
    rk = r_ref[...].astype(F32) * k_ref[...].astype(F32) * rk_ref[...]
    sums = _head_sum(jnp.concatenate([y, rk], axis=0), bd_ones)
    yc = y - sums[:TM] * (1.0 / hs)
    var = _head_sum(yc * yc, bd_ones) * (1.0 / hs)
    yn = yc * lax.rsqrt(var + GN_EPS) * lg_ref[...] + lb_ref[...]
    o_ref[...] = ((yn + sums[TM:] * v_ref[...].astype(F32)) * g_ref[...].astype(F32)).astype(o_ref.dtype)


def _rwkv_scan(r, lw, k, v, kk, b, g, rk, lg, lb, *, hs, tm):
    B, Lp, D = r.shape
    C = RWKV_CHUNK
    assert C == hs and V7X_MXU_EDGE % hs == 0 and D % V7X_MXU_EDGE == 0 and Lp % tm == 0 and tm % C == 0
    tile = pl.BlockSpec((None, tm, D), lambda bi, c: (bi, c, 0))
    return pl.pallas_call(
        functools.partial(_rwkv_kernel, hs=hs),
        grid=(B, Lp // tm),
        in_specs=[tile] * 7 + [_resident((1, D))] * 3,
        out_specs=tile,
        out_shape=jax.ShapeDtypeStruct((B, Lp, D), ACT_DTYPE),
        scratch_shapes=[pltpu.VMEM((D // V7X_MXU_EDGE, V7X_MXU_EDGE, V7X_MXU_EDGE), F32)],
        compiler_params=_params(("parallel", "arbitrary")),
        name="rwkv_scan",
    )(r, lw, k, v, kk, b, g, rk, lg, lb)


def _mixer_out_ffn_kernel(h_ref, yo_ref, u_ref, sgc_ref, sgr_ref, cw_ref, cb_ref, clg_ref, clb_ref, cwo_ref,
                          rwo_ref, wout_ref, g2_ref, b2_ref, wg_ref, wu_ref, wd_ref, g3_ref, b3_ref,
                          o_ref, ubuf, shbuf, cbuf, csbuf, h2buf, *, alpha, n_t):
    TM, D = h_ref.shape
    W = cw_ref.shape[0]
    s_id = pl.program_id(0)

    @pl.when(s_id == 0)
    def _():
        h2buf[...] = jnp.zeros_like(h2buf)

    @pl.when(s_id % n_t == 0)
    def _():
        ubuf[0:CONV_HALO, :] = jnp.zeros((CONV_HALO, D), F32)

    x = h2buf[...]
    xb = _mx(x)
    F = wg_ref.shape[1]
    ffn_acc = [None]
    order_zero = [jnp.zeros((1, V7X_LANES), F32)]

    def exact_zero(v):
        bits = pltpu.bitcast(v[0:V7X_SUBLANES, 0:V7X_LANES], jnp.int32)
        bits = lax.shift_right_logical(lax.shift_right_logical(bits, 16), 16)
        return bits[0:1, :].astype(F32)

    def ffn_job(c):
        fs = slice(c * FFN_CHUNK, (c + 1) * FFN_CHUNK)
        gate = _dot(xb, wg_ref[:, fs])
        up = _dot(xb, wu_ref[:, fs])
        part = _dot(_mx(gate * jax.nn.sigmoid(gate) * up), wd_ref[fs, :])
        ffn_acc[0] = part if ffn_acc[0] is None else ffn_acc[0] + part
        order_zero[0] = exact_zero(part)

    yr_box = [None]

    def yr_job():
        yr_box[0] = _dot(yo_ref[...], rwo_ref[...])
        order_zero[0] = exact_zero(yr_box[0])

    jobs = [yr_job] + [functools.partial(ffn_job, c) for c in range(F // FFN_CHUNK)]
    n_row_blocks = TM // CONV_ROWS
    n_blocks = (D // CONV_LANES + 1) * n_row_blocks
    done = [0]

    def run_jobs(block_idx):
        target = min(len(jobs), -(-(block_idx + 1) * len(jobs) // n_blocks))
        while done[0] < target:
            jobs[done[0]]()
            done[0] += 1

    ubuf[CONV_HALO:CONV_HALO + TM, :] = u_ref[...].astype(F32)
    base = CONV_HALO - (W - 1)
    n_rows = CONV_HALO + TM
    blk = (CONV_ROWS // V7X_SUBLANES, V7X_SUBLANES, CONV_LANES)
    run_jobs(0)
    for g in range(D // CONV_LANES):
        ln = slice(g * CONV_LANES, (g + 1) * CONV_LANES)
        window = ubuf[:, ln]
        for s in range(1, V7X_SUBLANES):
            shbuf[s - 1] = pltpu.roll(window, n_rows - s, 0)
        for i in range(n_row_blocks):
            acc = None
            for j in range(W):
                s = (base + j) % V7X_SUBLANES
                o = base + j - s + i * CONV_ROWS
                rows_j = ubuf[o:o + CONV_ROWS, ln] if s == 0 else shbuf[s - 1, o:o + CONV_ROWS, :]
                tap = cw_ref[j, :, ln]
                if j == 0:
                    tap = tap + jnp.concatenate([order_zero[0]] * (CONV_LANES // V7X_LANES), axis=1)
                term = rows_j.reshape(blk) * tap
                acc = term if acc is None else acc + term
            cbuf[i * CONV_ROWS:(i + 1) * CONV_ROWS, ln] = acc.reshape(CONV_ROWS, CONV_LANES)
            run_jobs(g * n_row_blocks + i + 1)
    ubuf[0:CONV_HALO, :] = ubuf[TM:TM + CONV_HALO, :]
    for i in range(n_row_blocks):
        rs = slice(i * CONV_ROWS, (i + 1) * CONV_ROWS)
        bias = cb_ref[...] + jnp.concatenate([order_zero[0]] * (D // V7X_LANES), axis=1)
        cn = _layer_norm(cbuf[rs, :] + bias, clg_ref[...], clb_ref[...])
        csbuf[rs, :] = _mx(cn * jax.nn.sigmoid(cn))
        run_jobs((D // CONV_LANES) * n_row_blocks + i + 1)
    assert done[0] == len(jobs)
    o_ref[...] = _layer_norm(alpha * x + 0.5 * ffn_acc[0], g3_ref[...], b3_ref[...])
    yc = _dot(csbuf[...], cwo_ref[...])

    hm = sgc_ref[...].astype(F32) * yc + sgr_ref[...].astype(F32) * yr_box[0]
    m = _dot(_mx(hm), wout_ref[...])
    h2buf[...] = _layer_norm(alpha * h_ref[...] + m, g2_ref[...], b2_ref[...])


def _mixer_out_ffn(h3d, yo, u, sgc, sgr, cw, cb, clg, clb, cwo, rwo, wout, g2, b2, wg, wu, wd, g3, b3, *, alpha, tm):
    B, Lp, D = h3d.shape
    W = cw.shape[0]
    n_t = Lp // tm
    n_tiles = B * n_t
    assert W - 1 <= CONV_HALO and tm % CONV_ROWS == 0 and tm >= CONV_HALO and D % CONV_LANES == 0
    assert yo.shape[2] == rwo.shape[0] and u.shape[2] == D and cwo.shape == (D, D) and wg.shape[1] % FFN_CHUNK == 0

    def tile_in(s):
        i = jnp.minimum(s, n_tiles - 1)
        return (i // n_t, i % n_t, 0)

    def tile_out(s):
        i = jnp.maximum(s - 1, 0)
        return (i // n_t, i % n_t, 0)

    t_in = lambda d: pl.BlockSpec((None, tm, d), tile_in)
    weights = [cw, cb, clg, clb, cwo, rwo, wout, g2, b2, wg, wu, wd, g3, b3]
    return pl.pallas_call(
        functools.partial(_mixer_out_ffn_kernel, alpha=alpha, n_t=n_t),
        grid=(n_tiles + 1,),
        in_specs=[t_in(D), t_in(yo.shape[2]), t_in(D), t_in(D), t_in(D)] + [_resident(w.shape) for w in weights],
        out_specs=pl.BlockSpec((None, tm, D), tile_out),
        out_shape=jax.ShapeDtypeStruct((B, Lp, D), F32),
        scratch_shapes=[pltpu.VMEM((CONV_HALO + tm, D), F32),
                        pltpu.VMEM((V7X_SUBLANES - 1, CONV_HALO + tm, CONV_LANES), F32),
                        pltpu.VMEM((tm, D), F32),
                        pltpu.VMEM((tm, D), MXU_DTYPE),
                        pltpu.VMEM((tm, D), F32)],
        compiler_params=pltpu.CompilerParams(dimension_semantics=("arbitrary",),
                                             vmem_limit_bytes=FUSED_VMEM_LIMIT_BYTES),
        name="mixer_out_ffn",
    )(h3d, yo, u, sgc, sgr, *weights)


def kernel(x, meta_tokens, ln1_g, ln1_b, ffn1_wg, ffn1_wu, ffn1_wd, w_in, time_mix, conv_w, conv_b, conv_ln_g, conv_ln_b, conv_wo, w0, w_up, a0, a_up, g_up, k_k, k_a, r_k, lnx_g, lnx_b, rwkv_wo, w_out, ln2_g, ln2_b, ffn2_wg, ffn2_wu, ffn2_wd, ln3_g, ln3_b):
    B, L, D = x.shape
    depth = ffn1_wg.shape[0]
    n_meta = meta_tokens.shape[0]
    hs = r_k.shape[2]
    rank_w = w_up.shape[1]
    alpha = (2 * depth) ** 0.25

    Lt = n_meta + L
    Lp = -(-Lt // (2 * RWKV_CHUNK)) * (2 * RWKV_CHUNK)
    tm_time = _largest_tile(Lp, 320, RWKV_CHUNK)
    tm_tok = _largest_tile(B * Lp, 512, 8)
    tm_rwkv = _largest_tile(Lp, RWKV_TILE, RWKV_CHUNK)

    meta = jnp.broadcast_to(meta_tokens.astype(x.dtype)[None], (B, n_meta, D))
    h = jnp.concatenate([meta, x, jnp.zeros((B, Lp - Lt, D), x.dtype)], axis=1)

    row = lambda p: p.reshape(1, -1).astype(F32)
    zeros_wa = lambda m: jnp.zeros_like(m)
    for l in range(depth):
        h = _ffn_ln(h.reshape(B * Lp, D), _mx(ffn1_wg[l]), _mx(ffn1_wu[l]), _mx(ffn1_wd[l]),
                    row(ln1_g[l]), row(ln1_b[l]), alpha=alpha, tm=tm_tok)
        wabd = jnp.concatenate(
            [jnp.concatenate([w_up[l], zeros_wa(w_up[l])], axis=1),
             jnp.concatenate([zeros_wa(a_up[l]), a_up[l]], axis=1)], axis=0)
        u, sgc, r, lw, k, v, kk, b, g, sgr = _mixer_in(
            h.reshape(B, Lp, D), _mx(w_in[l]), row(time_mix[l]), _mx(wabd), _mx(g_up[l]),
            row(w0[l]), row(a0[l]), row(k_k[l]), row(k_a[l]), tm=tm_time, hs=hs, rank_w=rank_w)
        yo = _rwkv_scan(r, lw, k, v, kk, b, g, row(r_k[l]), row(lnx_g[l]), row(lnx_b[l]), hs=hs, tm=tm_rwkv)
        taps = jnp.broadcast_to(conv_w[l].astype(F32)[:, None, :], (conv_w.shape[1], V7X_SUBLANES, conv_w.shape[2]))
        h = _mixer_out_ffn(h.reshape(B, Lp, D), yo, u, sgc, sgr, taps, row(conv_b[l]), row(conv_ln_g[l]),
                           row(conv_ln_b[l]), _mx(conv_wo[l]), _mx(rwkv_wo[l]), _mx(w_out[l]), row(ln2_g[l]),
                           row(ln2_b[l]), _mx(ffn2_wg[l]), _mx(ffn2_wu[l]), _mx(ffn2_wd[l]), row(ln3_g[l]),
                           row(ln3_b[l]), alpha=alpha, tm=tm_time)
    return h.reshape(B, Lp, D)[:, n_meta:n_meta + L]
```

```python
import functools
import math

import jax
import jax.numpy as jnp
from jax import lax
from jax.experimental import pallas as pl
from jax.experimental.pallas import tpu as pltpu

LN_EPS = 1e-5
GN_EPS = 64e-5
KK_NORM_FLOOR = 1e-12
DECAY_SCALE = math.exp(-0.5)

F32 = jnp.float32
MXU_DTYPE = jnp.bfloat16
ACT_DTYPE = jnp.bfloat16

V7X_LANES = 128
V7X_SUBLANES = 8
V7X_MXU_EDGE = 256
V7X_VMEM_BYTES = 64 * 1024 * 1024
VMEM_LIMIT_BYTES = 56 * 1024 * 1024
FUSED_VMEM_LIMIT_BYTES = 60 * 1024 * 1024

RWKV_CHUNK = 64
RWKV_TILE = 320
CONV_HALO = 32
CONV_ROWS = 64
CONV_LANES = 256
FFN_CHUNK = 256


def _dot(a, b):
    return jnp.dot(a, b, preferred_element_type=F32)


def _dot_nt(a, b):
    return lax.dot_general(a, b, (((1,), (1,)), ((), ())), preferred_element_type=F32)


def _dot_tn(a, b):
    return lax.dot_general(a, b, (((0,), (0,)), ((), ())), preferred_element_type=F32)


def _mx(x):
    return x.astype(MXU_DTYPE)


def _layer_norm(x, g, b, eps=LN_EPS):
    mu = jnp.mean(x, axis=-1, keepdims=True)
    xc = x - mu
    var = jnp.mean(xc * xc, axis=-1, keepdims=True)
    return xc * lax.rsqrt(var + eps) * g + b


def _largest_tile(n, cap, mult):
    best = None
    for t in range(mult, min(n, cap) + 1, mult):
        if n % t == 0:
            best = t
    assert best is not None, (n, cap, mult)
    return best


def _resident(shape):
    nd = len(shape)
    return pl.BlockSpec(shape, lambda *_: (0,) * nd, pipeline_mode=pl.Buffered(1))


def _params(sem):
    return pltpu.CompilerParams(dimension_semantics=sem, vmem_limit_bytes=VMEM_LIMIT_BYTES)


def _ffn_kernel(h_ref, wg_ref, wu_ref, wd_ref, g_ref, b_ref, o_ref, *, alpha):
    x = h_ref[...]
    xb = _mx(x)
    gate = _dot(xb, wg_ref[...])
    up = _dot(xb, wu_ref[...])
    act = _mx(gate * jax.nn.sigmoid(gate) * up)
    y = _dot(act, wd_ref[...])
    o_ref[...] = _layer_norm(alpha * x + 0.5 * y, g_ref[...], b_ref[...])


def _ffn_ln(h2d, wg, wu, wd, g, b, *, alpha, tm):
    T, D = h2d.shape
    F = wg.shape[1]
    row = pl.BlockSpec((tm, D), lambda i: (i, 0))
    return pl.pallas_call(
        functools.partial(_ffn_kernel, alpha=alpha),
        grid=(T // tm,),
        in_specs=[row, _resident((D, F)), _resident((D, F)), _resident((F, D)),
                  _resident((1, D)), _resident((1, D))],
        out_specs=row,
        out_shape=jax.ShapeDtypeStruct((T, D), F32),
        compiler_params=_params(("parallel",)),
        name="ffn_ln",
    )(h2d, wg, wu, wd, g, b)


def _head_sum(x, bd_ones):
    Q = bd_ones.shape[0]
    outs = [_dot(_mx(x[:, q * Q:(q + 1) * Q]), bd_ones) for q in range(x.shape[1] // Q)]
    return outs[0] if len(outs) == 1 else jnp.concatenate(outs, axis=1)


def _bd_mask(Q, hs):
    row = lax.broadcasted_iota(jnp.int32, (Q, Q), 0)
    col = lax.broadcasted_iota(jnp.int32, (Q, Q), 1)
    return (row // hs) == (col // hs)


def _mixer_in_kernel(h_ref, win_ref, tmix_ref, wabd_ref, gup_ref, w0_ref, a0_ref, kkw_ref, kaw_ref,
                     u_ref, sgc_ref, r_ref, lw_ref, k_ref, v_ref, kk_ref, b_ref, g_ref, sgr_ref,
                     zprev, *, d_conv, d_rwkv, rank_w, rank_wa, rank_g, hs):
    TM, D = h_ref.shape

    @pl.when(pl.program_id(1) == 0)
    def _():
        zprev[...] = jnp.zeros_like(zprev)

    hb = _mx(h_ref[...])
    c0 = 2 * d_conv
    c1 = c0 + 3 * d_rwkv + rank_wa + rank_g

    z = _dot(hb, win_ref[...])
    u_ref[...] = (z[:, :d_conv] * jax.nn.sigmoid(z[:, d_conv:c0])).astype(u_ref.dtype)
    sgc_ref[...] = jax.nn.sigmoid(z[:, c1:c1 + D]).astype(sgc_ref.dtype)
    sgr_ref[...] = jax.nn.sigmoid(z[:, c1 + D:c1 + 2 * D]).astype(sgr_ref.dtype)

    zr = z[:, c0:c1]
    rows = lax.broadcasted_iota(jnp.int32, zr.shape, 0)
    prev = jnp.where(rows == 0, zprev[V7X_SUBLANES - 1:V7X_SUBLANES, :], pltpu.roll(zr, 1, 0))
    zprev[...] = zr[TM - V7X_SUBLANES:TM, :]
    zr = zr + (prev - zr) * tmix_ref[...]
    r = zr[:, 0:d_rwkv]
    k = zr[:, d_rwkv:2 * d_rwkv]
    v = zr[:, 2 * d_rwkv:3 * d_rwkv]
    zwa = zr[:, 3 * d_rwkv:3 * d_rwkv + rank_wa]
    zg = zr[:, 3 * d_rwkv + rank_wa:3 * d_rwkv + rank_wa + rank_g]
    lanes = lax.broadcasted_iota(jnp.int32, zwa.shape, 1)
    xwa = jnp.where(lanes < rank_w, jnp.tanh(zwa), zwa)
    lora = _dot(_mx(xwa), wabd_ref[...])
    g = _dot(_mx(jax.nn.sigmoid(zg)), gup_ref[...])
    lw_ref[...] = -DECAY_SCALE * jax.nn.sigmoid(w0_ref[...] + lora[:, :d_rwkv])
    a = jax.nn.sigmoid(a0_ref[...] + lora[:, d_rwkv:])
    kk = k * kkw_ref[...]
    bd_ones = _bd_mask(V7X_MXU_EDGE, hs).astype(MXU_DTYPE)
    kk = kk * lax.rsqrt(jnp.maximum(_head_sum(kk * kk, bd_ones), KK_NORM_FLOOR * KK_NORM_FLOOR))
    r_ref[...] = r.astype(r_ref.dtype)
    k_ref[...] = (k * (1.0 + (a - 1.0) * kaw_ref[...])).astype(k_ref.dtype)
    v_ref[...] = v.astype(v_ref.dtype)
    kk_ref[...] = kk.astype(kk_ref.dtype)
    b_ref[...] = (kk * a).astype(b_ref.dtype)
    g_ref[...] = g.astype(g_ref.dtype)


def _mixer_in(h3d, win, tmix, wabd, gup, w0, a0, kkw, kaw, *, tm, hs, rank_w):
    B, Lp, D = h3d.shape
    d_rwkv = w0.shape[1]
    rank_wa = wabd.shape[0]
    rank_g = gup.shape[0]
    n_zr = 3 * d_rwkv + rank_wa + rank_g
    d_conv = (win.shape[1] - n_zr - 2 * D) // 2
    assert 2 * d_conv + n_zr + 2 * D == win.shape[1] and d_conv % V7X_LANES == 0
    assert rank_wa % V7X_LANES == 0 and rank_g % V7X_LANES == 0 and d_rwkv % V7X_MXU_EDGE == 0
    tile = lambda d: pl.BlockSpec((None, tm, d), lambda b, t: (b, t, 0))
    act = lambda d: jax.ShapeDtypeStruct((B, Lp, d), ACT_DTYPE)
    kern = functools.partial(_mixer_in_kernel, d_conv=d_conv, d_rwkv=d_rwkv, rank_w=rank_w,
                             rank_wa=rank_wa, rank_g=rank_g, hs=hs)
    return pl.pallas_call(
        kern,
        grid=(B, Lp // tm),
        in_specs=[tile(D), _resident(win.shape), _resident(tmix.shape),
                  _resident(wabd.shape), _resident(gup.shape), _resident(w0.shape), _resident(a0.shape),
                  _resident(kkw.shape), _resident(kaw.shape)],
        out_specs=[tile(d_conv), tile(D), tile(d_rwkv), tile(d_rwkv), tile(d_rwkv), tile(d_rwkv), tile(d_rwkv),
                   tile(d_rwkv), tile(d_rwkv), tile(D)],
        out_shape=[act(d_conv), act(D), act(d_rwkv), jax.ShapeDtypeStruct((B, Lp, d_rwkv), F32), act(d_rwkv),
                   act(d_rwkv), act(d_rwkv), act(d_rwkv), act(d_rwkv), act(D)],
        scratch_shapes=[pltpu.VMEM((V7X_SUBLANES, n_zr), F32)],
        compiler_params=_params(("parallel", "arbitrary")),
        name="mixer_in",
    )(h3d, win, tmix, wabd, gup, w0, a0, kkw, kaw)


def _chunk_cumsum(x, C):
    pos = lax.broadcasted_iota(jnp.int32, x.shape, 0) % C
    sh = 1
    while sh < C:
        x = x + jnp.where(pos >= sh, pltpu.roll(x, sh, 0), 0.0)
        sh *= 2
    return x


def _rwkv_kernel(r_ref, lw_ref, k_ref, v_ref, kk_ref, b_ref, g_ref, rk_ref, lg_ref, lb_ref,
                 o_ref, s_ref, *, hs):
    TM, D = r_ref.shape
    C = RWKV_CHUNK
    Q = V7X_MXU_EDGE
    HPQ = Q // hs
    n_chunks = TM // C
    n_quads = D // Q

    @pl.when(pl.program_id(1) == 0)
    def _():
        s_ref[...] = jnp.zeros_like(s_ref)

    bd = _bd_mask(Q, hs)
    trow = lax.broadcasted_iota(jnp.int32, (C, Q), 0)
    spos = lax.broadcasted_iota(jnp.int32, (C, Q), 1) % C
    strict = spos < trow
    incl = spos <= trow
    eye_cat = (spos == trow).astype(F32)
    zero = jnp.zeros((), MXU_DTYPE)

    def bdt(x):
        return jnp.where(bd, jnp.concatenate([_mx(x)] * HPQ, axis=0), zero)

    units = [(c, q) for c in range(n_chunks) for q in range(n_quads)]
    rows = lambda c: slice(c * C, (c + 1) * C)
    cols = lambda q: slice(q * Q, (q + 1) * Q)

    def scaled_operands(c):
        rc = r_ref[rows(c), :].astype(F32)
        kc = k_ref[rows(c), :].astype(F32)
        kkc = kk_ref[rows(c), :].astype(F32)
        bc = b_ref[rows(c), :].astype(F32)
        lw = lw_ref[rows(c), :]
        cum = _chunk_cumsum(lw, C)
        cum_last = cum[C - 1:C, :]
        e_neg = jnp.exp(-cum)
        e_rel = jnp.exp(cum_last - cum)
        return dict(a_t=_mx(-kkc * jnp.exp(cum - lw)), r_t=_mx(rc * jnp.exp(cum)), b_t=_mx(bc * e_neg),
                    k_t=_mx(kc * e_neg), b_g=_mx(bc * e_rel), k_g=_mx(kc * e_rel),
                    v_m=_mx(v_ref[rows(c), :]), g_c=jnp.exp(cum_last))

    ops, lhs, a_b, a_k = {}, {}, {}, {}
    for c in range(n_chunks):
        ops[c] = scaled_operands(c)
        for q in range(n_quads):
            u = (c, q)
            lhs[u] = jnp.concatenate([ops[c]["a_t"][:, cols(q)], ops[c]["r_t"][:, cols(q)]], axis=0)
            a_b[u] = _dot_nt(lhs[u], bdt(ops[c]["b_t"][:, cols(q)]))
            a_k[u] = _dot_nt(lhs[u], bdt(ops[c]["k_t"][:, cols(q)]))
    n_ab = {u: jnp.where(strict, a_b[u][:C], 0.0) for u in units}
    a_rb = {u: _mx(jnp.where(incl, a_b[u][C:], 0.0)) for u in units}
    both_mask = jnp.concatenate([strict, incl], axis=0)
    a_xk = {u: _mx(jnp.where(both_mask, a_k[u], 0.0)) for u in units}
    xkv = {u: _dot(a_xk[u], bdt(ops[u[0]]["v_m"][:, cols(u[1])])) for u in units}
    p = {u: eye_cat + n_ab[u] for u in units}
    npow = {u: _dot(_mx(n_ab[u]), bdt(n_ab[u])) for u in units}
    levels = C.bit_length() - 2
    for lvl in range(levels):
        if lvl + 1 < levels:
            both = {u: _dot(jnp.concatenate([_mx(p[u]), _mx(npow[u])], axis=0), bdt(npow[u])) for u in units}
            p = {u: p[u] + both[u][:C] for u in units}
            npow = {u: both[u][C:] for u in units}
        else:
            prod = {u: _dot(_mx(p[u]), bdt(npow[u])) for u in units}
            p = {u: p[u] + prod[u] for u in units}
    t_inv = {u: _mx(p[u]) for u in units}

    s_bd = [s_ref[q] for q in range(n_quads)]
    ys = []
    for c in range(n_chunks):
        qs = range(n_quads)
        sx = [_dot_nt(lhs[c, q], _mx(s_bd[q])) for q in qs]
        u_m = [_mx(_dot(t_inv[c, q], bdt(sx[q][:C] + xkv[c, q][:C]))) for q in qs]
        yu = [_dot(a_rb[c, q], bdt(u_m[q])) for q in qs]
        oc = ops[c]
        upd = [_dot_tn(jnp.concatenate([u_m[q], oc["v_m"][:, cols(q)]], axis=0),
                       jnp.concatenate([oc["b_g"][:, cols(q)], oc["k_g"][:, cols(q)]], axis=0)) for q in qs]
        s_bd = [s_bd[q] * oc["g_c"][:, cols(q)] + jnp.where(bd, upd[q], 0.0) for q in qs]
        ys.append(jnp.concatenate([sx[q][C:] + xkv[c, q][C:] + yu[q] for q in qs], axis=1))
    for q in range(n_quads):
        s_ref[q] = s_bd[q]
    y = jnp.concatenate(ys, axis=0)

    bd_ones = bd.astype(MXU_DTYPE)
    rk = r_ref[...].astype(F32) * k_ref[...].astype(F32) * rk_ref[...]
    sums = _head_sum(jnp.concatenate([y, rk], axis=0), bd_ones)
    yc = y - sums[:TM] * (1.0 / hs)
    var = _head_sum(yc * yc, bd_ones) * (1.0 / hs)
    yn = yc * lax.rsqrt(var + GN_EPS) * lg_ref[...] + lb_ref[...]
    o_ref[...] = ((yn + sums[TM:] * v_ref[...].astype(F32)) * g_ref[...].astype(F32)).astype(o_ref.dtype)


def _rwkv_scan(r, lw, k, v, kk, b, g, rk, lg, lb, *, hs, tm):
    B, Lp, D = r.shape
    C = RWKV_CHUNK
    assert C == hs and V7X_MXU_EDGE % hs == 0 and D % V7X_MXU_EDGE == 0 and Lp % tm == 0 and tm % C == 0
    tile = pl.BlockSpec((None, tm, D), lambda bi, c: (bi, c, 0))
    return pl.pallas_call(
        functools.partial(_rwkv_kernel, hs=hs),
        grid=(B, Lp // tm),
        in_specs=[tile] * 7 + [_resident((1, D))] * 3,
        out_specs=tile,
        out_shape=jax.ShapeDtypeStruct((B, Lp, D), ACT_DTYPE),
        scratch_shapes=[pltpu.VMEM((D // V7X_MXU_EDGE, V7X_MXU_EDGE, V7X_MXU_EDGE), F32)],
        compiler_params=_params(("parallel", "arbitrary")),
        name="rwkv_scan",
    )(r, lw, k, v, kk, b, g, rk, lg, lb)


def _mixer_out_ffn_kernel(h_ref, yo_ref, u_ref, sgc_ref, sgr_ref, cw_ref, cb_ref, clg_ref, clb_ref, cwo_ref,
                          rwo_ref, wout_ref, g2_ref, b2_ref, wg_ref, wu_ref, wd_ref, g3_ref, b3_ref,
                          o_ref, ubuf, shbuf, cbuf, csbuf, h2buf, *, alpha, n_t):
    TM, D = h_ref.shape
    W = cw_ref.shape[0]
    s_id = pl.program_id(0)

    @pl.when(s_id == 0)
    def _():
        h2buf[...] = jnp.zeros_like(h2buf)

    @pl.when(s_id % n_t == 0)
    def _():
        ubuf[0:CONV_HALO, :] = jnp.zeros((CONV_HALO, D), F32)

    x = h2buf[...]
    xb = _mx(x)
    F = wg_ref.shape[1]
    ffn_acc = [None]
    order_zero = [jnp.zeros((1, V7X_LANES), F32)]

    def exact_zero(v):
        bits = pltpu.bitcast(v[0:V7X_SUBLANES, 0:V7X_LANES], jnp.int32)
        bits = lax.shift_right_logical(lax.shift_right_logical(bits, 16), 16)
        return bits[0:1, :].astype(F32)

    def ffn_job(c):
        fs = slice(c * FFN_CHUNK, (c + 1) * FFN_CHUNK)
        gate = _dot(xb, wg_ref[:, fs])
        up = _dot(xb, wu_ref[:, fs])
        part = _dot(_mx(gate * jax.nn.sigmoid(gate) * up), wd_ref[fs, :])
        ffn_acc[0] = part if ffn_acc[0] is None else ffn_acc[0] + part
        order_zero[0] = exact_zero(part)

    yr_box = [None]

    def yr_job():
        yr_box[0] = _dot(yo_ref[...], rwo_ref[...])
        order_zero[0] = exact_zero(yr_box[0])

    jobs = [yr_job] + [functools.partial(ffn_job, c) for c in range(F // FFN_CHUNK)]
    n_row_blocks = TM // CONV_ROWS
    n_blocks = (D // CONV_LANES + 1) * n_row_blocks
    done = [0]

    def run_jobs(block_idx):
        target = min(len(jobs), -(-(block_idx + 1) * len(jobs) // n_blocks))
        while done[0] < target:
            jobs[done[0]]()
            done[0] += 1

    ubuf[CONV_HALO:CONV_HALO + TM, :] = u_ref[...].astype(F32)
    base = CONV_HALO - (W - 1)
    n_rows = CONV_HALO + TM
    blk = (CONV_ROWS // V7X_SUBLANES, V7X_SUBLANES, CONV_LANES)
    run_jobs(0)
    for g in range(D // CONV_LANES):
        ln = slice(g * CONV_LANES, (g + 1) * CONV_LANES)
        window = ubuf[:, ln]
        for s in range(1, V7X_SUBLANES):
            shbuf[s - 1] = pltpu.roll(window, n_rows - s, 0)
        for i in range(n_row_blocks):
            acc = None
            for j in range(W):
                s = (base + j) % V7X_SUBLANES
                o = base + j - s + i * CONV_ROWS
                rows_j = ubuf[o:o + CONV_ROWS, ln] if s == 0 else shbuf[s - 1, o:o + CONV_ROWS, :]
                tap = cw_ref[j, :, ln]
                if j == 0:
                    tap = tap + jnp.concatenate([order_zero[0]] * (CONV_LANES // V7X_LANES), axis=1)
                term = rows_j.reshape(blk) * tap
                acc = term if acc is None else acc + term
            cbuf[i * CONV_ROWS:(i + 1) * CONV_ROWS, ln] = acc.reshape(CONV_ROWS, CONV_LANES)
            run_jobs(g * n_row_blocks + i + 1)
    ubuf[0:CONV_HALO, :] = ubuf[TM:TM + CONV_HALO, :]
    for i in range(n_row_blocks):
        rs = slice(i * CONV_ROWS, (i + 1) * CONV_ROWS)
        bias = cb_ref[...] + jnp.concatenate([order_zero[0]] * (D // V7X_LANES), axis=1)
        cn = _layer_norm(cbuf[rs, :] + bias, clg_ref[...], clb_ref[...])
        csbuf[rs, :] = _mx(cn * jax.nn.sigmoid(cn))
        run_jobs((D // CONV_LANES) * n_row_blocks + i + 1)
    assert done[0] == len(jobs)
    o_ref[...] = _layer_norm(alpha * x + 0.5 * ffn_acc[0], g3_ref[...], b3_ref[...])
    yc = _dot(csbuf[...], cwo_ref[...])

    hm = sgc_ref[...].astype(F32) * yc + sgr_ref[...].astype(F32) * yr_box[0]
    m = _dot(_mx(hm), wout_ref[...])
    h2buf[...] = _layer_norm(alpha * h_ref[...] + m, g2_ref[...], b2_ref[...])


def _mixer_out_ffn(h3d, yo, u, sgc, sgr, cw, cb, clg, clb, cwo, rwo, wout, g2, b2, wg, wu, wd, g3, b3, *, alpha, tm):
    B, Lp, D = h3d.shape
    W = cw.shape[0]
    n_t = Lp // tm
    n_tiles = B * n_t
    assert W - 1 <= CONV_HALO and tm % CONV_ROWS == 0 and tm >= CONV_HALO and D % CONV_LANES == 0
    assert yo.shape[2] == rwo.shape[0] and u.shape[2] == D and cwo.shape == (D, D) and wg.shape[1] % FFN_CHUNK == 0

    def tile_in(s):
        i = jnp.minimum(s, n_tiles - 1)
        return (i // n_t, i % n_t, 0)

    def tile_out(s):
        i = jnp.maximum(s - 1, 0)
        return (i // n_t, i % n_t, 0)

    t_in = lambda d: pl.BlockSpec((None, tm, d), tile_in)
    weights = [cw, cb, clg, clb, cwo, rwo, wout, g2, b2, wg, wu, wd, g3, b3]
    return pl.pallas_call(
        functools.partial(_mixer_out_ffn_kernel, alpha=alpha, n_t=n_t),
        grid=(n_tiles + 1,),
        in_specs=[t_in(D), t_in(yo.shape[2]), t_in(D), t_in(D), t_in(D)] + [_resident(w.shape) for w in weights],
        out_specs=pl.BlockSpec((None, tm, D), tile_out),
        out_shape=jax.ShapeDtypeStruct((B, Lp, D), F32),
        scratch_shapes=[pltpu.VMEM((CONV_HALO + tm, D), F32),
                        pltpu.VMEM((V7X_SUBLANES - 1, CONV_HALO + tm, CONV_LANES), F32),
                        pltpu.VMEM((tm, D), F32),
                        pltpu.VMEM((tm, D), MXU_DTYPE),
                        pltpu.VMEM((tm, D), F32)],
        compiler_params=pltpu.CompilerParams(dimension_semantics=("arbitrary",),
                                             vmem_limit_bytes=FUSED_VMEM_LIMIT_BYTES),
        name="mixer_out_ffn",
    )(h3d, yo, u, sgc, sgr, *weights)


def kernel(x, meta_tokens, ln1_g, ln1_b, ffn1_wg, ffn1_wu, ffn1_wd, w_in, time_mix, conv_w, conv_b, conv_ln_g, conv_ln_b, conv_wo, w0, w_up, a0, a_up, g_up, k_k, k_a, r_k, lnx_g, lnx_b, rwkv_wo, w_out, ln2_g, ln2_b, ffn2_wg, ffn2_wu, ffn2_wd, ln3_g, ln3_b):
    B, L, D = x.shape
    depth = ffn1_wg.shape[0]
    n_meta = meta_tokens.shape[0]
    hs = r_k.shape[2]
    rank_w = w_up.shape[1]
    alpha = (2 * depth) ** 0.25

    Lt = n_meta + L
    Lp = -(-Lt // (2 * RWKV_CHUNK)) * (2 * RWKV_CHUNK)
    tm_time = _largest_tile(Lp, 320, RWKV_CHUNK)
    tm_tok = _largest_tile(B * Lp, 512, 8)
    tm_rwkv = _largest_tile(Lp, RWKV_TILE, RWKV_CHUNK)

    meta = jnp.broadcast_to(meta_tokens.astype(x.dtype)[None], (B, n_meta, D))
    h = jnp.concatenate([meta, x, jnp.zeros((B, Lp - Lt, D), x.dtype)], axis=1)

    row = lambda p: p.reshape(1, -1).astype(F32)
    zeros_wa = lambda m: jnp.zeros_like(m)
    for l in range(depth):
        h = _ffn_ln(h.reshape(B * Lp, D), _mx(ffn1_wg[l]), _mx(ffn1_wu[l]), _mx(ffn1_wd[l]),
                    row(ln1_g[l]), row(ln1_b[l]), alpha=alpha, tm=tm_tok)
        wabd = jnp.concatenate(
            [jnp.concatenate([w_up[l], zeros_wa(w_up[l])], axis=1),
             jnp.concatenate([zeros_wa(a_up[l]), a_up[l]], axis=1)], axis=0)
        u, sgc, r, lw, k, v, kk, b, g, sgr = _mixer_in(
            h.reshape(B, Lp, D), _mx(w_in[l]), row(time_mix[l]), _mx(wabd), _mx(g_up[l]),
            row(w0[l]), row(a0[l]), row(k_k[l]), row(k_a[l]), tm=tm_time, hs=hs, rank_w=rank_w)
        yo = _rwkv_scan(r, lw, k, v, kk, b, g, row(r_k[l]), row(lnx_g[l]), row(lnx_b[l]), hs=hs, tm=tm_rwkv)
        taps = jnp.broadcast_to(conv_w[l].astype(F32)[:, None, :], (conv_w.shape[1], V7X_SUBLANES, conv_w.shape[2]))
        h = _mixer_out_ffn(h.reshape(B, Lp, D), yo, u, sgc, sgr, taps, row(conv_b[l]), row(conv_ln_g[l]),
                           row(conv_ln_b[l]), _mx(conv_wo[l]), _mx(rwkv_wo[l]), _mx(w_out[l]), row(ln2_g[l]),
                           row(ln2_b[l]), _mx(ffn2_wg[l]), _mx(ffn2_wu[l]), _mx(ffn2_wd[l]), row(ln3_g[l]),
                           row(ln3_b[l]), alpha=alpha, tm=tm_time)
    return h.reshape(B, Lp, D)[:, n_meta:n_meta + L]
```
